```python
import jax, jax.numpy as jnp
from jax import lax
import numpy as np

D_MODEL = 1024
BATCH = 8
SEQ = 8192
DEPTH = 4

N_A_LAYERS = DEPTH // 2
N_B_LAYERS = DEPTH - N_A_LAYERS
D_FF = 2816
FFN_HALF = 0.5
EPS = 1e-6
MEM_TOKENS = 256
MEM_HEADS = 4
MEM_HEAD_DIM = 64
MEM_WIDTH = MEM_HEADS * MEM_HEAD_DIM
MIX_WIDTH = D_MODEL - MEM_WIDTH
CONV_CH = MIX_WIDTH
CONV_WIDTH = 3
MLA_HEADS = 6
QK_NOPE_DIM = 128
QK_ROPE_DIM = 64
QK_HEAD_DIM = QK_NOPE_DIM + QK_ROPE_DIM
V_HEAD_DIM = 128
Q_LORA_RANK = 384
KV_LORA_RANK = 256
ROPE_THETA = 10000.0
Q_BLOCK = 128

kernel_name = "yoco_shortconv_mla_macaron_memory"


def rms_norm(x, g):
    xf = x.astype(jnp.float32)
    y = xf * lax.rsqrt(jnp.mean(xf * xf, axis=-1, keepdims=True) + EPS)
    return (y * g.astype(jnp.float32)).astype(x.dtype)


def swiglu(h, w13, w2):
    gate, up = jnp.split(h @ w13, 2, axis=-1)
    return (jax.nn.silu(gate) * up) @ w2


def rope_tables(positions):
    inv_freq = ROPE_THETA ** (-jnp.arange(0, QK_ROPE_DIM, 2, dtype=jnp.float32) / QK_ROPE_DIM)
    ang = positions.astype(jnp.float32)[..., None] * inv_freq
    return jnp.cos(ang), jnp.sin(ang)


def apply_rope(x, cos, sin):
    x1, x2 = jnp.split(x, 2, axis=-1)
    return jnp.concatenate([x1 * cos - x2 * sin, x2 * cos + x1 * sin], axis=-1).astype(x.dtype)


def causal_short_conv(u, w):
    seq = u.shape[1]
    up = jnp.pad(u, ((0, 0), (CONV_WIDTH - 1, 0), (0, 0)))
    y = w[0] * up[:, 0:seq]
    for tap in range(1, CONV_WIDTH):
        y = y + w[tap] * up[:, tap:tap + seq]
    return y


def short_conv_mixer(h, w_in, conv_w):
    proj = h @ w_in
    gate_b, gate_c, xt, q_mem = jnp.split(proj, [CONV_CH, 2 * CONV_CH, 3 * CONV_CH], axis=-1)
    y = gate_b * causal_short_conv(gate_c * xt, conv_w)
    return y, q_mem


def memory_attention(q_mem, mem, mem_norm_g, w_mem_kv, g_q, g_k):
    b, s, _ = q_mem.shape
    q = rms_norm(q_mem.reshape(b, s, MEM_HEADS, MEM_HEAD_DIM), g_q)
    k, v = jnp.split(rms_norm(mem, mem_norm_g) @ w_mem_kv, 2, axis=-1)
    m = mem.shape[1]
    k = rms_norm(k.reshape(b, m, MEM_HEADS, MEM_HEAD_DIM), g_k)
    v = v.reshape(b, m, MEM_HEADS, MEM_HEAD_DIM)
    scores = jnp.einsum('bqhd,bmhd->bhqm', q, k).astype(jnp.float32) * (MEM_HEAD_DIM ** -0.5)
    p = jax.nn.softmax(scores, axis=-1).astype(v.dtype)
    o = jnp.einsum('bhqm,bmhd->bqhd', p, v)
    return o.reshape(b, s, MEM_WIDTH)


def shared_mla_kv(x, kv_norm_g, w_dkv, g_ckv, w_ukv, w_kr, g_k_nope, g_k_rope, cos, sin):
    b, s, _ = x.shape
    h = rms_norm(x, kv_norm_g)
    c_kv = rms_norm(h @ w_dkv, g_ckv)
    kv = (c_kv @ w_ukv).reshape(b, s, MLA_HEADS, QK_NOPE_DIM + V_HEAD_DIM)
    k_nope = rms_norm(kv[..., :QK_NOPE_DIM], g_k_nope)
    v = kv[..., QK_NOPE_DIM:]
    k_rope = apply_rope(rms_norm(h @ w_kr, g_k_rope), cos, sin)
    return k_nope, k_rope, v


def mla_queries(h, w_in, g_q_lora, w_uq, g_q_nope, g_q_rope, cos, sin):
    b, s, _ = h.shape
    c_q, q_mem = jnp.split(h @ w_in, [Q_LORA_RANK], axis=-1)
    q = (rms_norm(c_q, g_q_lora) @ w_uq).reshape(b, s, MLA_HEADS, QK_HEAD_DIM)
    q_nope = rms_norm(q[..., :QK_NOPE_DIM], g_q_nope)
    q_rope = apply_rope(rms_norm(q[..., QK_NOPE_DIM:], g_q_rope), cos[:, :, None, :], sin[:, :, None, :])
    return q_nope, q_rope, q_mem


def causal_mla_attention(q_nope, q_rope, k_nope, k_rope, v):
    b, seq = q_nope.shape[0], q_nope.shape[1]
    scale = QK_HEAD_DIM ** -0.5
    outs = []
    for blk in range(seq // Q_BLOCK):
        start, end = blk * Q_BLOCK, (blk + 1) * Q_BLOCK
        qn, qr = q_nope[:, start:end], q_rope[:, start:end]
        scores = (jnp.einsum('bqhd,bkhd->bhqk', qn, k_nope[:, :end])
                  + jnp.einsum('bqhd,bkd->bhqk', qr, k_rope[:, :end])).astype(jnp.float32) * scale
        mask = (start + jnp.arange(Q_BLOCK))[:, None] >= jnp.arange(end)[None, :]
        scores = jnp.where(mask, scores, -jnp.inf)
        p = jax.nn.softmax(scores, axis=-1).astype(v.dtype)
        outs.append(jnp.einsum('bhqk,bkhd->bqhd', p, v[:, :end]))
    o = jnp.concatenate(outs, axis=1)
    return o.reshape(b, seq, MLA_HEADS * V_HEAD_DIM)


def setup_inputs(seed: int = 0) -> dict:
    key = jax.random.key(seed)
    ks = jax.random.split(key, 26)
    f32 = jnp.float32

    def w(k, shape, fan_in):
        return jax.random.normal(k, shape, f32) * (fan_in ** -0.5)

    def g(k, shape):
        return 1.0 + 0.02 * jax.random.normal(k, shape, f32)

    offsets = jax.random.randint(ks[2], (BATCH, 1), 0, 1024, dtype=jnp.int32)
    positions = (offsets + jnp.arange(SEQ, dtype=jnp.int32)[None, :]).astype(jnp.int32)
    return {
        'x': jax.random.normal(ks[0], (BATCH, SEQ, D_MODEL), f32),
        'mem': jax.random.normal(ks[1], (BATCH, MEM_TOKENS, D_MODEL), f32),
        'positions': positions,
        'norm_g': g(ks[3], (DEPTH, 3, D_MODEL)),
        'ffn_w13': w(ks[4], (DEPTH, 2, D_MODEL, 2 * D_FF), D_MODEL),
        'ffn_w2': w(ks[5], (DEPTH, 2, D_FF, D_MODEL), D_FF),
        'w_out': w(ks[6], (DEPTH, D_MODEL, D_MODEL), D_MODEL),
        'mem_norm_g': g(ks[7], (DEPTH, D_MODEL)),
        'w_mem_kv': w(ks[8], (DEPTH, D_MODEL, 2 * MEM_WIDTH), D_MODEL),
        'g_mem_q': g(ks[9], (DEPTH, MEM_HEAD_DIM)),
        'g_mem_k': g(ks[10], (DEPTH, MEM_HEAD_DIM)),
        'conv_w_in': w(ks[11], (N_A_LAYERS, D_MODEL, 3 * CONV_CH + MEM_WIDTH), D_MODEL),
        'conv_w': w(ks[12], (N_A_LAYERS, CONV_WIDTH, CONV_CH), CONV_WIDTH),
        'mla_w_in': w(ks[13], (N_B_LAYERS, D_MODEL, Q_LORA_RANK + MEM_WIDTH), D_MODEL),
        'g_q_lora': g(ks[14], (N_B_LAYERS, Q_LORA_RANK)),
        'w_uq': w(ks[15], (N_B_LAYERS, Q_LORA_RANK, MLA_HEADS * QK_HEAD_DIM), Q_LORA_RANK),
        'g_q_nope': g(ks[16], (N_B_LAYERS, QK_NOPE_DIM)),
        'g_q_rope': g(ks[17], (N_B_LAYERS, QK_ROPE_DIM)),
        'kv_norm_g': g(ks[18], (D_MODEL,)),
        'w_dkv': w(ks[19], (D_MODEL, KV_LORA_RANK), D_MODEL),
        'g_ckv': g(ks[20], (KV_LORA_RANK,)),
        'w_ukv': w(ks[21], (KV_LORA_RANK, MLA_HEADS * (QK_NOPE_DIM + V_HEAD_DIM)), KV_LORA_RANK),
        'w_kr': w(ks[22], (D_MODEL, QK_ROPE_DIM), D_MODEL),
        'g_k_nope': g(ks[23], (QK_NOPE_DIM,)),
        'g_k_rope': g(ks[24], (QK_ROPE_DIM,)),
    }


def reference(x, mem, positions, norm_g, ffn_w13, ffn_w2, w_out, mem_norm_g, w_mem_kv, g_mem_q, g_mem_k,
              conv_w_in, conv_w, mla_w_in, g_q_lora, w_uq, g_q_nope, g_q_rope,
              kv_norm_g, w_dkv, g_ckv, w_ukv, w_kr, g_k_nope, g_k_rope):
    cos, sin = rope_tables(positions)
    shared = None
    for layer in range(DEPTH):
        if layer == N_A_LAYERS:
            shared = shared_mla_kv(x, kv_norm_g, w_dkv, g_ckv, w_ukv, w_kr, g_k_nope, g_k_rope, cos, sin)
        x = x + FFN_HALF * swiglu(rms_norm(x, norm_g[layer, 0]), ffn_w13[layer, 0], ffn_w2[layer, 0])
        h = rms_norm(x, norm_g[layer, 1])
        if layer < N_A_LAYERS:
            y_mix, q_mem = short_conv_mixer(h, conv_w_in[layer], conv_w[layer])
        else:
            j = layer - N_A_LAYERS
            q_nope, q_rope, q_mem = mla_queries(h, mla_w_in[j], g_q_lora[j], w_uq[j], g_q_nope[j], g_q_rope[j], cos, sin)
            k_nope, k_rope, v = shared
            y_mix = causal_mla_attention(q_nope, q_rope, k_nope, k_rope, v)
        y_mem = memory_attention(q_mem, mem, mem_norm_g[layer], w_mem_kv[layer], g_mem_q[layer], g_mem_k[layer])
        x = x + jnp.concatenate([y_mix, y_mem], axis=-1) @ w_out[layer]
        x = x + FFN_HALF * swiglu(rms_norm(x, norm_g[layer, 2]), ffn_w13[layer, 1], ffn_w2[layer, 1])
    return x
```

```python
import functools
import math

import jax
import jax.numpy as jnp
from jax import lax
from jax.experimental import pallas as pl
from jax.experimental.pallas import tpu as pltpu

F32 = jnp.float32
BF16 = jnp.bfloat16

EPS = 1e-6
FFN_HALF = 0.5
MEM_HEADS = 4
MEM_HEAD_DIM = 64
MEM_WIDTH = MEM_HEADS * MEM_HEAD_DIM
CONV_WIDTH = 3
MLA_HEADS = 6
QK_NOPE_DIM = 128
QK_ROPE_DIM = 64
ROPE_HALF = QK_ROPE_DIM // 2
QK_HEAD_DIM = QK_NOPE_DIM + QK_ROPE_DIM
V_HEAD_DIM = 128
ROPE_THETA = 10000.0

LANES = 128
SUBLANES = 8
QK_PAD_DIM = 2 * LANES
Q_GROUP = 3 * LANES
V7X_VMEM_BYTES = 64 * 1024 * 1024
VMEM_LIMIT_BYTES = V7X_VMEM_BYTES - 8 * 1024 * 1024
MASK_VALUE = -1e30

TOKEN_TILE = 512
FLASH_BLOCK = 512


def _compiler_params(n_axes):
    return pltpu.CompilerParams(
        dimension_semantics=("arbitrary",) * n_axes, vmem_limit_bytes=VMEM_LIMIT_BYTES)


def _resident(shape):
    zeros = (0,) * len(shape)
    return pl.BlockSpec(shape, lambda *_: zeros, pipeline_mode=pl.Buffered(1))


def _dot(a, b):
    return jnp.dot(a, b, preferred_element_type=F32)


def _dot_nt(a, b):
    return lax.dot_general(a, b, (((1,), (1,)), ((), ())), preferred_element_type=F32)


def _rms(x, g):
    return x * lax.rsqrt(jnp.mean(x * x, axis=-1, keepdims=True) + EPS) * g


def _group_sum(sq, ones_bd):
    hi = sq.astype(BF16)
    lo = (sq - hi.astype(F32)).astype(BF16)
    return _dot(hi, ones_bd) + _dot(lo, ones_bd)


def _rope_tables_kernel(pos_ref, inv_freq_ref, cos_ref, sin_ref):
    ang = pos_ref[...].astype(F32) * inv_freq_ref[...]
    cos_ref[...] = jnp.cos(ang)
    sin_ref[...] = jnp.sin(ang)


def _rope_tables(positions, tm):
    t = positions.size
    pos = positions.reshape(t, 1)
    inv_freq = ROPE_THETA ** (-jnp.arange(0, QK_ROPE_DIM, 2, dtype=F32) / QK_ROPE_DIM)
    inv_freq = jnp.tile(inv_freq, LANES // ROPE_HALF).reshape(1, LANES)
    return pl.pallas_call(
        _rope_tables_kernel,
        grid=(t // tm,),
        in_specs=[pl.BlockSpec((tm, 1), lambda i: (i, 0)), _resident((1, LANES))],
        out_specs=[pl.BlockSpec((tm, LANES), lambda i: (i, 0))] * 2,
        out_shape=[jax.ShapeDtypeStruct((t, LANES), F32)] * 2,
        compiler_params=_compiler_params(1),
        name="rope_tables",
    )(pos, inv_freq)


def _mem_kv_kernel(mem_ref, g_ref, w_ref, gk_ref, ones_ref, kbd_ref, vbd_ref):
    m = mem_ref.shape[1]
    hm = _rms(mem_ref[0], g_ref[0]).astype(BF16)
    kv = _dot(hm, w_ref[0])
    k = kv[:, :MEM_WIDTH]
    v = kv[:, MEM_WIDTH:]
    ms = _group_sum(k * k, ones_ref[...]) * (1.0 / MEM_HEAD_DIM)
    kn = k * lax.rsqrt(ms + EPS) * gk_ref[0] * (MEM_HEAD_DIM ** -0.5)
    head_of_lane = lax.broadcasted_iota(jnp.int32, (m, MEM_WIDTH), 1) // MEM_HEAD_DIM
    for h in range(MEM_HEADS):
        sel = head_of_lane == h
        kbd_ref[0, 0, h * m:(h + 1) * m, :] = jnp.where(sel, kn, 0.0).astype(BF16)
        vbd_ref[0, 0, h * m:(h + 1) * m, :] = jnp.where(sel, v, 0.0).astype(BF16)


def _mem_kv(mem, mem_norm_g, w_mem_kv, g_mem_k, ones_bd):
    b, m, d = mem.shape
    depth = w_mem_kv.shape[0]
    gk = jnp.tile(g_mem_k, (1, MEM_HEADS)).reshape(depth, 1, MEM_WIDTH)
    out = jax.ShapeDtypeStruct((depth, b, MEM_HEADS * m, MEM_WIDTH), BF16)
    out_spec = pl.BlockSpec((1, 1, MEM_HEADS * m, MEM_WIDTH), lambda l, i: (l, i, 0, 0))
    return pl.pallas_call(
        _mem_kv_kernel,
        grid=(depth, b),
        in_specs=[
            pl.BlockSpec((1, m, d), lambda l, i: (i, 0, 0)),
            pl.BlockSpec((1, 1, d), lambda l, i: (l, 0, 0)),
            pl.BlockSpec((1, d, 2 * MEM_WIDTH), lambda l, i: (l, 0, 0)),
            pl.BlockSpec((1, 1, MEM_WIDTH), lambda l, i: (l, 0, 0)),
            _resident((MEM_WIDTH, MEM_WIDTH)),
        ],
        out_specs=[out_spec, out_spec],
        out_shape=[out, out],
        compiler_params=_compiler_params(2),
        name="mem_kv",
    )(mem, mem_norm_g.reshape(depth, 1, d), w_mem_kv.astype(BF16), gk, ones_bd)


def _memory_attention(q, kbd, vbd, gq, ones_bd):
    tm = q.shape[0]
    m = kbd.shape[0] // MEM_HEADS
    ms = _group_sum(q * q, ones_bd) * (1.0 / MEM_HEAD_DIM)
    qn = (q * lax.rsqrt(ms + EPS) * gq).astype(BF16)
    s = _dot_nt(qn, kbd)
    probs = []
    denoms = []
    for h in range(MEM_HEADS):
        sh = s[:, h * m:(h + 1) * m]
        p = jnp.exp(sh - jnp.max(sh, axis=-1, keepdims=True))
        denoms.append(jnp.sum(p, axis=-1, keepdims=True))
        probs.append(p.astype(BF16))
    o = _dot(jnp.concatenate(probs, axis=-1), vbd)
    head_of_lane = lax.broadcasted_iota(jnp.int32, (tm, MEM_WIDTH), 1) // MEM_HEAD_DIM
    denom = denoms[MEM_HEADS - 1]
    for h in range(MEM_HEADS - 2, -1, -1):
        denom = jnp.where(head_of_lane == h, denoms[h], denom)
    return o / denom


def _swiglu_step(x, g_ref, w13_ref, w2_ref):
    d_ff = w2_ref.shape[0]
    h = _rms(x, g_ref[...]).astype(BF16)
    gu = _dot(h, w13_ref[...])
    gate = gu[:, :d_ff]
    up = gu[:, d_ff:]
    act = (gate * jax.nn.sigmoid(gate) * up).astype(BF16)
    return x + FFN_HALF * _dot(act, w2_ref[...])


def _ffn_kernel(x_ref, g_ref, w13_ref, w2_ref, o_ref):
    o_ref[...] = _swiglu_step(x_ref[...], g_ref, w13_ref, w2_ref)


def _proj_ffn_kernel(x_ref, ymix_ref, ymem_ref, wout_ref, g_ref, w13_ref, w2_ref, o_ref):
    mix = ymix_ref.shape[1]
    x1 = (x_ref[...] + _dot(ymix_ref[...], wout_ref[:mix, :])
          + _dot(ymem_ref[...], wout_ref[mix:, :]))
    o_ref[...] = _swiglu_step(x1, g_ref, w13_ref, w2_ref)


def _ffn(x, g, w13, w2, tm):
    t, d = x.shape
    d_ff = w2.shape[0]
    row = pl.BlockSpec((tm, d), lambda i: (i, 0))
    return pl.pallas_call(
        _ffn_kernel,
        grid=(t // tm,),
        in_specs=[row, _resident((1, d)), _resident((d, 2 * d_ff)), _resident((d_ff, d))],
        out_specs=row,
        out_shape=jax.ShapeDtypeStruct((t, d), F32),
        compiler_params=_compiler_params(1),
        name="ffn",
    )(x, g.reshape(1, d), w13, w2)


def _proj_ffn(x, ymix, ymem, w_out, g, w13, w2, tm):
    t, d = x.shape
    d_ff = w2.shape[0]
    mix = ymix.shape[1]
    row = pl.BlockSpec((tm, d), lambda i: (i, 0))
    return pl.pallas_call(
        _proj_ffn_kernel,
        grid=(t // tm,),
        in_specs=[
            row,
            pl.BlockSpec((tm, mix), lambda i: (i, 0)),
            pl.BlockSpec((tm, d - mix), lambda i: (i, 0)),
            _resident((d, d)), _resident((1, d)), _resident((d, 2 * d_ff)), _resident((d_ff, d)),
        ],
        out_specs=row,
        out_shape=jax.ShapeDtypeStruct((t, d), F32),
        compiler_params=_compiler_params(1),
        name="proj_ffn",
    )(x, ymix, ymem, w_out, g.reshape(1, d), w13, w2)


def _conv_mixer_kernel(x_ref, g_ref, win_ref, convw_ref, kbd_ref, vbd_ref, gq_ref, ones_ref,
                       ymix_ref, ymem_ref, carry_ref):
    tm = x_ref.shape[1]
    ch = ymix_ref.shape[2]

    @pl.when(pl.program_id(1) == 0)
    def _():
        carry_ref[...] = jnp.zeros_like(carry_ref)

    h = _rms(x_ref[0], g_ref[...]).astype(BF16)
    proj = _dot(h, win_ref[...])
    gate_b = proj[:, :ch]
    u = proj[:, ch:2 * ch] * proj[:, 2 * ch:3 * ch]
    q_mem = proj[:, 3 * ch:]

    prev2 = carry_ref[SUBLANES - 2:SUBLANES - 1, :]
    prev1 = carry_ref[SUBLANES - 1:SUBLANES, :]
    row = lax.broadcasted_iota(jnp.int32, (tm, ch), 0)
    u1 = jnp.where(row == 0, prev1, pltpu.roll(u, 1, 0))
    u2 = jnp.where(row == 0, prev2, jnp.where(row == 1, prev1, pltpu.roll(u, 2, 0)))
    carry_ref[...] = u[tm - SUBLANES:, :]

    w = convw_ref[...]
    conv = w[0:1, :] * u2 + w[1:2, :] * u1 + w[2:3, :] * u
    ymix_ref[0] = (gate_b * conv).astype(BF16)
    ymem_ref[0] = _memory_attention(
        q_mem, kbd_ref[0, 0], vbd_ref[0, 0], gq_ref[...], ones_ref[...]).astype(BF16)


def _conv_mixer(x, g, w_in, conv_w, kbd, vbd, layer, gq, ones_bd, tm):
    b, s, d = x.shape
    ch = conv_w.shape[1]
    n_in = w_in.shape[1]
    mem_rows = kbd.shape[2]
    mem_spec = pl.BlockSpec((1, 1, mem_rows, MEM_WIDTH), lambda i, j: (layer, i, 0, 0))
    return pl.pallas_call(
        _conv_mixer_kernel,
        grid=(b, s // tm),
        in_specs=[
            pl.BlockSpec((1, tm, d), lambda i, j: (i, j, 0)),
            _resident((1, d)), _resident((d, n_in)), _resident((CONV_WIDTH, ch)),
            mem_spec, mem_spec, _resident((1, MEM_WIDTH)), _resident((MEM_WIDTH, MEM_WIDTH)),
        ],
        out_specs=[
            pl.BlockSpec((1, tm, ch), lambda i, j: (i, j, 0)),
            pl.BlockSpec((1, tm, MEM_WIDTH), lambda i, j: (i, j, 0)),
        ],
        out_shape=[
            jax.ShapeDtypeStruct((b, s, ch), BF16),
            jax.ShapeDtypeStruct((b, s, MEM_WIDTH), BF16),
        ],
        scratch_shapes=[pltpu.VMEM((SUBLANES, ch), F32)],
        compiler_params=_compiler_params(2),
        name="conv_mixer",
    )(x, g.reshape(1, d), w_in, conv_w, kbd, vbd, gq, ones_bd)


def _rope_rows(r, r_swapped, g, g_swapped, cos, sin):
    inv = lax.rsqrt(jnp.sum(r * r, axis=-1, keepdims=True) * (1.0 / QK_ROPE_DIM) + EPS)
    return (r * g * cos + r_swapped * g_swapped * sin) * inv


def _kv_prep_kernel(x_ref, g_ref, wd_ref, gckv_ref, wukv_ref, gkn_ref, gkr_ref, gkrs_ref,
                    cos_ref, sin_ref, k_ref, v_ref):
    lora = gckv_ref.shape[1]
    h = _rms(x_ref[0], g_ref[...]).astype(BF16)
    d = _dot(h, wd_ref[...])
    ckv = _rms(d[:, :lora], gckv_ref[...]).astype(BF16)
    kr = _rope_rows(d[:, lora:lora + LANES], d[:, lora + LANES:], gkr_ref[...], gkrs_ref[...],
                    cos_ref[0], sin_ref[0]).astype(BF16)
    kv = _dot(ckv, wukv_ref[...])
    for hd in range(MLA_HEADS):
        base = hd * (QK_NOPE_DIM + V_HEAD_DIM)
        kn = _rms(kv[:, base:base + QK_NOPE_DIM], gkn_ref[...])
        k_ref[0, hd, :, :QK_NOPE_DIM] = kn.astype(BF16)
        k_ref[0, hd, :, QK_NOPE_DIM:] = kr
        v_ref[0, hd] = kv[:, base + QK_NOPE_DIM:base + QK_NOPE_DIM + V_HEAD_DIM].astype(BF16)


def _swap_rope_cols(w):
    pad = jnp.zeros(w.shape[:-1] + (LANES - QK_ROPE_DIM,), w.dtype)
    x1, x2 = w[..., :ROPE_HALF], w[..., ROPE_HALF:]
    return jnp.concatenate([w, pad], axis=-1), jnp.concatenate([-x2, x1, pad], axis=-1)


def _swap_rope_gain(g):
    pad = jnp.zeros((LANES - QK_ROPE_DIM,), g.dtype)
    straight = jnp.concatenate([g, pad])
    swapped = jnp.concatenate([g[ROPE_HALF:], g[:ROPE_HALF], pad])
    return straight.reshape(1, LANES), swapped.reshape(1, LANES)


def _kv_prep(x, kv_norm_g, w_dkv, g_ckv, w_ukv, w_kr, g_k_nope, g_k_rope, cos, sin, tm):
    b, s, d = x.shape
    lora = w_dkv.shape[1]
    kr_cols, kr_swapped = _swap_rope_cols(w_kr)
    wd = jnp.concatenate([w_dkv, kr_cols, kr_swapped], axis=-1).astype(BF16)
    gkr, gkrs = _swap_rope_gain(g_k_rope)
    tab = pl.BlockSpec((1, tm, LANES), lambda i, j: (i, j, 0))
    return pl.pallas_call(
        _kv_prep_kernel,
        grid=(b, s // tm),
        in_specs=[
            pl.BlockSpec((1, tm, d), lambda i, j: (i, j, 0)),
            _resident((1, d)), _resident((d, lora + 2 * LANES)), _resident((1, lora)),
            _resident(w_ukv.shape), _resident((1, QK_NOPE_DIM)),
            _resident((1, LANES)), _resident((1, LANES)), tab, tab,
        ],
        out_specs=[
            pl.BlockSpec((1, MLA_HEADS, tm, QK_PAD_DIM), lambda i, j: (i, 0, j, 0)),
            pl.BlockSpec((1, MLA_HEADS, tm, V_HEAD_DIM), lambda i, j: (i, 0, j, 0)),
        ],
        out_shape=[
            jax.ShapeDtypeStruct((b, MLA_HEADS, s, QK_PAD_DIM), BF16),
            jax.ShapeDtypeStruct((b, MLA_HEADS, s, V_HEAD_DIM), BF16),
        ],
        compiler_params=_compiler_params(2),
        name="kv_prep",
    )(x, kv_norm_g.reshape(1, d), wd, g_ckv.reshape(1, lora), w_ukv.astype(BF16),
      g_k_nope.reshape(1, QK_NOPE_DIM), gkr, gkrs, cos, sin)


def _q_prep_kernel(x_ref, g_ref, win_ref, gql_ref, wuq_ref, gqn_ref, gqr_ref, gqrs_ref,
                   cos_ref, sin_ref, kbd_ref, vbd_ref, gq_ref, ones_ref, q_ref, ymem_ref,
                   *, score_scale):
    lora = gql_ref.shape[1]
    h = _rms(x_ref[0], g_ref[...]).astype(BF16)
    proj = _dot(h, win_ref[...])
    cq = _rms(proj[:, :lora], gql_ref[...]).astype(BF16)
    qall = _dot(cq, wuq_ref[...])
    cos = cos_ref[0]
    sin = sin_ref[0]
    for hd in range(MLA_HEADS):
        base = hd * Q_GROUP
        qn = _rms(qall[:, base:base + LANES], gqn_ref[...])
        qr = _rope_rows(qall[:, base + LANES:base + 2 * LANES], qall[:, base + 2 * LANES:base + Q_GROUP],
                        gqr_ref[...], gqrs_ref[...], cos, sin)
        q_ref[0, hd, :, :QK_NOPE_DIM] = (qn * score_scale).astype(BF16)
        q_ref[0, hd, :, QK_NOPE_DIM:] = (qr * score_scale).astype(BF16)
    ymem_ref[0] = _memory_attention(
        proj[:, lora:], kbd_ref[0, 0], vbd_ref[0, 0], gq_ref[...], ones_ref[...]).astype(BF16)


def _q_prep(x, g, w_in, g_q_lora, w_uq, g_q_nope, g_q_rope, cos, sin, kbd, vbd, layer, gq,
            ones_bd, tm):
    b, s, d = x.shape
    lora = g_q_lora.shape[0]
    w = w_uq.reshape(lora, MLA_HEADS, QK_HEAD_DIM)
    rope_cols, rope_swapped = _swap_rope_cols(w[..., QK_NOPE_DIM:])
    wuq = jnp.concatenate([w[..., :QK_NOPE_DIM], rope_cols, rope_swapped], axis=-1)
    wuq = wuq.reshape(lora, MLA_HEADS * Q_GROUP).astype(BF16)
    gqr, gqrs = _swap_rope_gain(g_q_rope)
    mem_rows = kbd.shape[2]
    mem_spec = pl.BlockSpec((1, 1, mem_rows, MEM_WIDTH), lambda i, j: (layer, i, 0, 0))
    tab = pl.BlockSpec((1, tm, LANES), lambda i, j: (i, j, 0))
    score_scale = QK_HEAD_DIM ** -0.5 * math.log2(math.e)
    return pl.pallas_call(
        functools.partial(_q_prep_kernel, score_scale=score_scale),
        grid=(b, s // tm),
        in_specs=[
            pl.BlockSpec((1, tm, d), lambda i, j: (i, j, 0)),
            _resident((1, d)), _resident(w_in.shape), _resident((1, lora)),
            _resident((lora, MLA_HEADS * Q_GROUP)), _resident((1, QK_NOPE_DIM)),
            _resident((1, LANES)), _resident((1, LANES)), tab, tab,
            mem_spec, mem_spec, _resident((1, MEM_WIDTH)), _resident((MEM_WIDTH, MEM_WIDTH)),
        ],
        out_specs=[
            pl.BlockSpec((1, MLA_HEADS, tm, QK_PAD_DIM), lambda i, j: (i, 0, j, 0)),
            pl.BlockSpec((1, tm, MEM_WIDTH), lambda i, j: (i, j, 0)),
        ],
        out_shape=[
            jax.ShapeDtypeStruct((b, MLA_HEADS, s, QK_PAD_DIM), BF16),
            jax.ShapeDtypeStruct((b, s, MEM_WIDTH), BF16),
        ],
        compiler_params=_compiler_params(2),
        name="q_prep",
    )(x, g.reshape(1, d), w_in, g_q_lora.reshape(1, lora), wuq, g_q_nope.reshape(1, QK_NOPE_DIM),
      gqr, gqrs, cos, sin, kbd, vbd, gq, ones_bd)


def _flash_kernel(q_ref, k_ref, v_ref, o_ref, m_ref, l_ref, acc_ref):
    blk = q_ref.shape[2]
    qi = pl.program_id(2)
    q = q_ref[0, 0]
    m_ref[...] = jnp.full_like(m_ref, MASK_VALUE)
    l_ref[...] = jnp.zeros_like(l_ref)
    acc_ref[...] = jnp.zeros_like(acc_ref)

    def step(j, diagonal):
        start = pl.multiple_of(j * blk, blk)
        s = _dot_nt(q, k_ref[0, 0, pl.ds(start, blk), :])
        if diagonal:
            row = lax.broadcasted_iota(jnp.int32, (blk, blk), 0)
            col = lax.broadcasted_iota(jnp.int32, (blk, blk), 1)
            s = jnp.where(row >= col, s, MASK_VALUE)
        m_prev = m_ref[...]
        m_new = jnp.maximum(m_prev, jnp.max(s, axis=-1, keepdims=True))
        alpha = jnp.exp2(m_prev - m_new)
        p = jnp.exp2(s - m_new)
        l_ref[...] = alpha * l_ref[...] + jnp.sum(p, axis=-1, keepdims=True)
        acc_ref[...] = alpha * acc_ref[...] + _dot(p.astype(BF16), v_ref[0, 0, pl.ds(start, blk), :])
        m_ref[...] = m_new

    def full_block(j, carry):
        step(j, diagonal=False)
        return carry

    lax.fori_loop(0, qi, full_block, 0)
    step(qi, diagonal=True)
    o_ref[0] = (acc_ref[...] / l_ref[...]).astype(BF16)


def _flash(q, k, v, blk):
    b, heads, s, _ = q.shape
    return pl.pallas_call(
        _flash_kernel,
        grid=(b, heads, s // blk),
        in_specs=[
            pl.BlockSpec((1, 1, blk, QK_PAD_DIM), lambda i, h, j: (i, h, j, 0)),
            pl.BlockSpec((1, 1, s, QK_PAD_DIM), lambda i, h, j: (i, h, 0, 0)),
            pl.BlockSpec((1, 1, s, V_HEAD_DIM), lambda i, h, j: (i, h, 0, 0)),
        ],
        out_specs=pl.BlockSpec((1, blk, V_HEAD_DIM), lambda i, h, j: (i, j, h)),
        out_shape=jax.ShapeDtypeStruct((b, s, heads * V_HEAD_DIM), BF16),
        scratch_shapes=[
            pltpu.VMEM((blk, 1), F32), pltpu.VMEM((blk, 1), F32), pltpu.VMEM((blk, V_HEAD_DIM), F32),
        ],
        compiler_params=_compiler_params(3),
        name="flash",
    )(q, k, v)


def kernel(x, mem, positions, norm_g, ffn_w13, ffn_w2, w_out, mem_norm_g, w_mem_kv, g_mem_q, g_mem_k,
           conv_w_in, conv_w, mla_w_in, g_q_lora, w_uq, g_q_nope, g_q_rope,
           kv_norm_g, w_dkv, g_ckv, w_ukv, w_kr, g_k_nope, g_k_rope):
    b, s, d = x.shape
    depth = norm_g.shape[0]
    n_conv = conv_w_in.shape[0]
    t = b * s
    tm = min(TOKEN_TILE, s)
    blk = min(FLASH_BLOCK, s)
    assert s % tm == 0 and s % blk == 0 and d - conv_w.shape[2] == MEM_WIDTH

    head_of = jnp.arange(MEM_WIDTH, dtype=jnp.int32) // MEM_HEAD_DIM
    ones_bd = (head_of[:, None] == head_of[None, :]).astype(BF16)
    kbd, vbd = _mem_kv(mem, mem_norm_g, w_mem_kv, g_mem_k, ones_bd)
    gq_all = jnp.tile(g_mem_q, (1, MEM_HEADS)).reshape(depth, 1, MEM_WIDTH)
    cos, sin = _rope_tables(positions, tm)
    cos = cos.reshape(b, s, LANES)
    sin = sin.reshape(b, s, LANES)

    w13 = ffn_w13.astype(BF16)
    w2 = ffn_w2.astype(BF16)
    wo = w_out.astype(BF16)

    xf = x.reshape(t, d)
    shared = None
    for layer in range(depth):
        if layer == n_conv:
            shared = _kv_prep(xf.reshape(b, s, d), kv_norm_g, w_dkv, g_ckv, w_ukv, w_kr, g_k_nope,
                              g_k_rope, cos, sin, tm)
        xf = _ffn(xf, norm_g[layer, 0], w13[layer, 0], w2[layer, 0], tm)
        x3 = xf.reshape(b, s, d)
        if layer < n_conv:
            ymix, ymem = _conv_mixer(x3, norm_g[layer, 1], conv_w_in[layer].astype(BF16), conv_w[layer],
                                     kbd, vbd, layer, gq_all[layer], ones_bd, tm)
        else:
            j = layer - n_conv
            q, ymem = _q_prep(x3, norm_g[layer, 1], mla_w_in[j].astype(BF16), g_q_lora[j], w_uq[j],
                              g_q_nope[j], g_q_rope[j], cos, sin, kbd, vbd, layer, gq_all[layer],
                              ones_bd, tm)
            ymix = _flash(q, shared[0], shared[1], blk)
        xf = _proj_ffn(xf, ymix.reshape(t, -1), ymem.reshape(t, MEM_WIDTH), wo[layer],
                       norm_g[layer, 2], w13[layer, 1], w2[layer, 1], tm)
    return xf.reshape(b, s, d)
```

```python
import functools
import math

import jax
import jax.numpy as jnp
from jax import lax
from jax.experimental import pallas as pl
from jax.experimental.pallas import tpu as pltpu

F32 = jnp.float32
BF16 = jnp.bfloat16

EPS = 1e-6
FFN_HALF = 0.5
MEM_HEADS = 4
MEM_HEAD_DIM = 64
MEM_WIDTH = MEM_HEADS * MEM_HEAD_DIM
CONV_WIDTH = 3
MLA_HEADS = 6
QK_NOPE_DIM = 128
QK_ROPE_DIM = 64
ROPE_HALF = QK_ROPE_DIM // 2
QK_HEAD_DIM = QK_NOPE_DIM + QK_ROPE_DIM
V_HEAD_DIM = 128
ROPE_THETA = 10000.0

LANES = 128
SUBLANES = 8
QK_PAD_DIM = 2 * LANES
Q_GROUP = 3 * LANES
BF16_SUBLANES = 2 * SUBLANES
VT_ROWS = V_HEAD_DIM + BF16_SUBLANES
V7X_VMEM_BYTES = 64 * 1024 * 1024
VMEM_LIMIT_BYTES = V7X_VMEM_BYTES - 8 * 1024 * 1024
MASK_VALUE = -1e30

TOKEN_TILE = 512
FLASH_BLOCK = 1024


def _compiler_params(n_axes):
    return pltpu.CompilerParams(
        dimension_semantics=("arbitrary",) * n_axes, vmem_limit_bytes=VMEM_LIMIT_BYTES)


def _resident(shape):
    zeros = (0,) * len(shape)
    return pl.BlockSpec(shape, lambda *_: zeros, pipeline_mode=pl.Buffered(1))


def _dot(a, b):
    return jnp.dot(a, b, preferred_element_type=F32)


def _dot_nt(a, b):
    return lax.dot_general(a, b, (((1,), (1,)), ((), ())), preferred_element_type=F32)


def _rms(x, g):
    return x * lax.rsqrt(jnp.mean(x * x, axis=-1, keepdims=True) + EPS) * g


def _group_sum(sq, ones_bd):
    hi = sq.astype(BF16)
    lo = (sq - hi.astype(F32)).astype(BF16)
    return _dot(hi, ones_bd) + _dot(lo, ones_bd)


def _rope_tables_kernel(pos_ref, inv_freq_ref, cos_ref, sin_ref):
    ang = pos_ref[...].astype(F32) * inv_freq_ref[...]
    cos_ref[...] = jnp.cos(ang)
    sin_ref[...] = jnp.sin(ang)


def _rope_tables(positions, tm):
    t = positions.size
    pos = positions.reshape(t, 1)
    inv_freq = ROPE_THETA ** (-jnp.arange(0, QK_ROPE_DIM, 2, dtype=F32) / QK_ROPE_DIM)
    inv_freq = jnp.tile(inv_freq, LANES // ROPE_HALF).reshape(1, LANES)
    return pl.pallas_call(
        _rope_tables_kernel,
        grid=(t // tm,),
        in_specs=[pl.BlockSpec((tm, 1), lambda i: (i, 0)), _resident((1, LANES))],
        out_specs=[pl.BlockSpec((tm, LANES), lambda i: (i, 0))] * 2,
        out_shape=[jax.ShapeDtypeStruct((t, LANES), F32)] * 2,
        compiler_params=_compiler_params(1),
        name="rope_tables",
    )(pos, inv_freq)


def _mem_kv_kernel(mem_ref, g_ref, w_ref, gk_ref, ones_ref, kbd_ref, vbd_ref):
    m = mem_ref.shape[1]
    hm = _rms(mem_ref[0], g_ref[0]).astype(BF16)
    kv = _dot(hm, w_ref[0])
    k = kv[:, :MEM_WIDTH]
    v = kv[:, MEM_WIDTH:]
    ms = _group_sum(k * k, ones_ref[...]) * (1.0 / MEM_HEAD_DIM)
    kn = k * lax.rsqrt(ms + EPS) * gk_ref[0] * (MEM_HEAD_DIM ** -0.5)
    head_of_lane = lax.broadcasted_iota(jnp.int32, (m, MEM_WIDTH), 1) // MEM_HEAD_DIM
    for h in range(MEM_HEADS):
        sel = head_of_lane == h
        kbd_ref[0, 0, h * m:(h + 1) * m, :] = jnp.where(sel, kn, 0.0).astype(BF16)
        vbd_ref[0, 0, h * m:(h + 1) * m, :] = jnp.where(sel, v, 0.0).astype(BF16)


def _mem_kv(mem, mem_norm_g, w_mem_kv, g_mem_k, ones_bd):
    b, m, d = mem.shape
    depth = w_mem_kv.shape[0]
    gk = jnp.tile(g_mem_k, (1, MEM_HEADS)).reshape(depth, 1, MEM_WIDTH)
    out = jax.ShapeDtypeStruct((depth, b, MEM_HEADS * m, MEM_WIDTH), BF16)
    out_spec = pl.BlockSpec((1, 1, MEM_HEADS * m, MEM_WIDTH), lambda l, i: (l, i, 0, 0))
    return pl.pallas_call(
        _mem_kv_kernel,
        grid=(depth, b),
        in_specs=[
            pl.BlockSpec((1, m, d), lambda l, i: (i, 0, 0)),
            pl.BlockSpec((1, 1, d), lambda l, i: (l, 0, 0)),
            pl.BlockSpec((1, d, 2 * MEM_WIDTH), lambda l, i: (l, 0, 0)),
            pl.BlockSpec((1, 1, MEM_WIDTH), lambda l, i: (l, 0, 0)),
            _resident((MEM_WIDTH, MEM_WIDTH)),
        ],
        out_specs=[out_spec, out_spec],
        out_shape=[out, out],
        compiler_params=_compiler_params(2),
        name="mem_kv",
    )(mem, mem_norm_g.reshape(depth, 1, d), w_mem_kv.astype(BF16), gk, ones_bd)


def _memory_attention(q, kbd, vbd, gq, ones_bd):
    tm = q.shape[0]
    m = kbd.shape[0] // MEM_HEADS
    ms = _group_sum(q * q, ones_bd) * (1.0 / MEM_HEAD_DIM)
    qn = (q * lax.rsqrt(ms + EPS) * gq).astype(BF16)
    s = _dot_nt(qn, kbd)
    probs = []
    denoms = []
    for h in range(MEM_HEADS):
        sh = s[:, h * m:(h + 1) * m]
        p = jnp.exp(sh - jnp.max(sh, axis=-1, keepdims=True))
        denoms.append(jnp.sum(p, axis=-1, keepdims=True))
        probs.append(p.astype(BF16))
    o = _dot(jnp.concatenate(probs, axis=-1), vbd)
    head_of_lane = lax.broadcasted_iota(jnp.int32, (tm, MEM_WIDTH), 1) // MEM_HEAD_DIM
    denom = denoms[MEM_HEADS - 1]
    for h in range(MEM_HEADS - 2, -1, -1):
        denom = jnp.where(head_of_lane == h, denoms[h], denom)
    return o / denom


def _swiglu_step(x, g_ref, w13_ref, w2_ref):
    d_ff = w2_ref.shape[0]
    h = _rms(x, g_ref[...]).astype(BF16)
    gu = _dot(h, w13_ref[...])
    gate = gu[:, :d_ff]
    up = gu[:, d_ff:]
    act = (gate * jax.nn.sigmoid(gate) * up).astype(BF16)
    return x + FFN_HALF * _dot(act, w2_ref[...])


def _ffn_kernel(x_ref, g_ref, w13_ref, w2_ref, o_ref):
    o_ref[...] = _swiglu_step(x_ref[...], g_ref, w13_ref, w2_ref)


def _proj_ffn_kernel(x_ref, ymix_ref, ymem_ref, wout_ref, g_ref, w13_ref, w2_ref, o_ref):
    mix = ymix_ref.shape[1]
    x1 = (x_ref[...] + _dot(ymix_ref[...], wout_ref[:mix, :])
          + _dot(ymem_ref[...], wout_ref[mix:, :]))
    o_ref[...] = _swiglu_step(x1, g_ref, w13_ref, w2_ref)


def _ffn(x, g, w13, w2, tm):
    t, d = x.shape
    d_ff = w2.shape[0]
    row = pl.BlockSpec((tm, d), lambda i: (i, 0))
    return pl.pallas_call(
        _ffn_kernel,
        grid=(t // tm,),
        in_specs=[row, _resident((1, d)), _resident((d, 2 * d_ff)), _resident((d_ff, d))],
        out_specs=row,
        out_shape=jax.ShapeDtypeStruct((t, d), F32),
        compiler_params=_compiler_params(1),
        name="ffn",
    )(x, g.reshape(1, d), w13, w2)


def _proj_ffn(x, ymix, ymem, w_out, g, w13, w2, tm):
    t, d = x.shape
    d_ff = w2.shape[0]
    mix = ymix.shape[1]
    row = pl.BlockSpec((tm, d), lambda i: (i, 0))
    return pl.pallas_call(
        _proj_ffn_kernel,
        grid=(t // tm,),
        in_specs=[
            row,
            pl.BlockSpec((tm, mix), lambda i: (i, 0)),
            pl.BlockSpec((tm, d - mix), lambda i: (i, 0)),
            _resident((d, d)), _resident((1, d)), _resident((d, 2 * d_ff)), _resident((d_ff, d)),
        ],
        out_specs=row,
        out_shape=jax.ShapeDtypeStruct((t, d), F32),
        compiler_params=_compiler_params(1),
        name="proj_ffn",
    )(x, ymix, ymem, w_out, g.reshape(1, d), w13, w2)


def _conv_mixer_kernel(x_ref, g_ref, win_ref, convw_ref, kbd_ref, vbd_ref, gq_ref, ones_ref,
                       ymix_ref, ymem_ref, carry_ref):
    tm = x_ref.shape[1]
    ch = ymix_ref.shape[2]

    @pl.when(pl.program_id(1) == 0)
    def _():
        carry_ref[...] = jnp.zeros_like(carry_ref)

    h = _rms(x_ref[0], g_ref[...]).astype(BF16)
    proj = _dot(h, win_ref[...])
    gate_b = proj[:, :ch]
    u = proj[:, ch:2 * ch] * proj[:, 2 * ch:3 * ch]
    q_mem = proj[:, 3 * ch:]

    prev2 = carry_ref[SUBLANES - 2:SUBLANES - 1, :]
    prev1 = carry_ref[SUBLANES - 1:SUBLANES, :]
    row = lax.broadcasted_iota(jnp.int32, (tm, ch), 0)
    u1 = jnp.where(row == 0, prev1, pltpu.roll(u, 1, 0))
    u2 = jnp.where(row == 0, prev2, jnp.where(row == 1, prev1, pltpu.roll(u, 2, 0)))
    carry_ref[...] = u[tm - SUBLANES:, :]

    w = convw_ref[...]
    conv = w[0:1, :] * u2 + w[1:2, :] * u1 + w[2:3, :] * u
    ymix_ref[0] = (gate_b * conv).astype(BF16)
    ymem_ref[0] = _memory_attention(
        q_mem, kbd_ref[0, 0], vbd_ref[0, 0], gq_ref[...], ones_ref[...]).astype(BF16)


def _conv_mixer(x, g, w_in, conv_w, kbd, vbd, layer, gq, ones_bd, tm):
    b, s, d = x.shape
    ch = conv_w.shape[1]
    n_in = w_in.shape[1]
    mem_rows = kbd.shape[2]
    mem_spec = pl.BlockSpec((1, 1, mem_rows, MEM_WIDTH), lambda i, j: (layer, i, 0, 0))
    return pl.pallas_call(
        _conv_mixer_kernel,
        grid=(b, s // tm),
        in_specs=[
            pl.BlockSpec((1, tm, d), lambda i, j: (i, j, 0)),
            _resident((1, d)), _resident((d, n_in)), _resident((CONV_WIDTH, ch)),
            mem_spec, mem_spec, _resident((1, MEM_WIDTH)), _resident((MEM_WIDTH, MEM_WIDTH)),
        ],
        out_specs=[
            pl.BlockSpec((1, tm, ch), lambda i, j: (i, j, 0)),
            pl.BlockSpec((1, tm, MEM_WIDTH), lambda i, j: (i, j, 0)),
        ],
        out_shape=[
            jax.ShapeDtypeStruct((b, s, ch), BF16),
            jax.ShapeDtypeStruct((b, s, MEM_WIDTH), BF16),
        ],
        scratch_shapes=[pltpu.VMEM((SUBLANES, ch), F32)],
        compiler_params=_compiler_params(2),
        name="conv_mixer",
    )(x, g.reshape(1, d), w_in, conv_w, kbd, vbd, gq, ones_bd)


def _rope_rows(r, r_swapped, g, g_swapped, cos, sin):
    inv = lax.rsqrt(jnp.sum(r * r, axis=-1, keepdims=True) * (1.0 / QK_ROPE_DIM) + EPS)
    return (r * g * cos + r_swapped * g_swapped * sin) * inv


def _kv_prep_kernel(x_ref, g_ref, wd_ref, gckv_ref, wukv_ref, gkn_ref, gkr_ref, gkrs_ref,
                    cos_ref, sin_ref, k_ref, vt_ref):
    tm = x_ref.shape[1]
    lora = gckv_ref.shape[1]
    ones_rows = jnp.where(
        lax.broadcasted_iota(jnp.int32, (VT_ROWS - V_HEAD_DIM, tm), 0) == 0, 1.0, 0.0)
    h = _rms(x_ref[0], g_ref[...]).astype(BF16)
    d = _dot(h, wd_ref[...])
    ckv = _rms(d[:, :lora], gckv_ref[...]).astype(BF16)
    kr = _rope_rows(d[:, lora:lora + LANES], d[:, lora + LANES:], gkr_ref[...], gkrs_ref[...],
                    cos_ref[0], sin_ref[0]).astype(BF16)
    kv = _dot(ckv, wukv_ref[...])
    for hd in range(MLA_HEADS):
        base = hd * (QK_NOPE_DIM + V_HEAD_DIM)
        kn = _rms(kv[:, base:base + QK_NOPE_DIM], gkn_ref[...])
        k_ref[0, hd, :, :QK_NOPE_DIM] = kn.astype(BF16)
        k_ref[0, hd, :, QK_NOPE_DIM:] = kr
        v = kv[:, base + QK_NOPE_DIM:base + QK_NOPE_DIM + V_HEAD_DIM]
        vt_ref[0, hd, 0] = jnp.concatenate([v.T, ones_rows], axis=0).astype(BF16)


def _swap_rope_cols(w):
    pad = jnp.zeros(w.shape[:-1] + (LANES - QK_ROPE_DIM,), w.dtype)
    x1, x2 = w[..., :ROPE_HALF], w[..., ROPE_HALF:]
    return jnp.concatenate([w, pad], axis=-1), jnp.concatenate([-x2, x1, pad], axis=-1)


def _swap_rope_gain(g):
    pad = jnp.zeros((LANES - QK_ROPE_DIM,), g.dtype)
    straight = jnp.concatenate([g, pad])
    swapped = jnp.concatenate([g[ROPE_HALF:], g[:ROPE_HALF], pad])
    return straight.reshape(1, LANES), swapped.reshape(1, LANES)


def _kv_prep(x, kv_norm_g, w_dkv, g_ckv, w_ukv, w_kr, g_k_nope, g_k_rope, cos, sin, tm, blk):
    b, s, d = x.shape
    lora = w_dkv.shape[1]
    per_blk = blk // tm
    kr_cols, kr_swapped = _swap_rope_cols(w_kr)
    wd = jnp.concatenate([w_dkv, kr_cols, kr_swapped], axis=-1).astype(BF16)
    gkr, gkrs = _swap_rope_gain(g_k_rope)
    tab = pl.BlockSpec((1, tm, LANES), lambda i, j: (i, j, 0))
    return pl.pallas_call(
        _kv_prep_kernel,
        grid=(b, s // tm),
        in_specs=[
            pl.BlockSpec((1, tm, d), lambda i, j: (i, j, 0)),
            _resident((1, d)), _resident((d, lora + 2 * LANES)), _resident((1, lora)),
            _resident(w_ukv.shape), _resident((1, QK_NOPE_DIM)),
            _resident((1, LANES)), _resident((1, LANES)), tab, tab,
        ],
        out_specs=[
            pl.BlockSpec((1, MLA_HEADS, tm, QK_PAD_DIM), lambda i, j: (i, 0, j, 0)),
            pl.BlockSpec((1, MLA_HEADS, 1, VT_ROWS, tm),
                         lambda i, j: (i, 0, j // per_blk, 0, j % per_blk)),
        ],
        out_shape=[
            jax.ShapeDtypeStruct((b, MLA_HEADS, s, QK_PAD_DIM), BF16),
            jax.ShapeDtypeStruct((b, MLA_HEADS, s // blk, VT_ROWS, blk), BF16),
        ],
        compiler_params=_compiler_params(2),
        name="kv_prep",
    )(x, kv_norm_g.reshape(1, d), wd, g_ckv.reshape(1, lora), w_ukv.astype(BF16),
      g_k_nope.reshape(1, QK_NOPE_DIM), gkr, gkrs, cos, sin)


def _q_prep_kernel(x_ref, g_ref, win_ref, gql_ref, wuq_ref, gqn_ref, gqr_ref, gqrs_ref,
                   cos_ref, sin_ref, kbd_ref, vbd_ref, gq_ref, ones_ref, q_ref, ymem_ref,
                   *, score_scale):
    lora = gql_ref.shape[1]
    h = _rms(x_ref[0], g_ref[...]).astype(BF16)
    proj = _dot(h, win_ref[...])
    cq = _rms(proj[:, :lora], gql_ref[...]).astype(BF16)
    qall = _dot(cq, wuq_ref[...])
    cos = cos_ref[0]
    sin = sin_ref[0]
    for hd in range(MLA_HEADS):
        base = hd * Q_GROUP
        qn = _rms(qall[:, base:base + LANES], gqn_ref[...])
        qr = _rope_rows(qall[:, base + LANES:base + 2 * LANES], qall[:, base + 2 * LANES:base + Q_GROUP],
                        gqr_ref[...], gqrs_ref[...], cos, sin)
        q_ref[0, hd, :, :QK_NOPE_DIM] = (qn * score_scale).astype(BF16)
        q_ref[0, hd, :, QK_NOPE_DIM:] = (qr * score_scale).astype(BF16)
    ymem_ref[0] = _memory_attention(
        proj[:, lora:], kbd_ref[0, 0], vbd_ref[0, 0], gq_ref[...], ones_ref[...]).astype(BF16)


def _q_prep(x, g, w_in, g_q_lora, w_uq, g_q_nope, g_q_rope, cos, sin, kbd, vbd, layer, gq,
            ones_bd, tm):
    b, s, d = x.shape
    lora = g_q_lora.shape[0]
    w = w_uq.reshape(lora, MLA_HEADS, QK_HEAD_DIM)
    rope_cols, rope_swapped = _swap_rope_cols(w[..., QK_NOPE_DIM:])
    wuq = jnp.concatenate([w[..., :QK_NOPE_DIM], rope_cols, rope_swapped], axis=-1)
    wuq = wuq.reshape(lora, MLA_HEADS * Q_GROUP).astype(BF16)
    gqr, gqrs = _swap_rope_gain(g_q_rope)
    mem_rows = kbd.shape[2]
    mem_spec = pl.BlockSpec((1, 1, mem_rows, MEM_WIDTH), lambda i, j: (layer, i, 0, 0))
    tab = pl.BlockSpec((1, tm, LANES), lambda i, j: (i, j, 0))
    score_scale = QK_HEAD_DIM ** -0.5 * math.log2(math.e)
    return pl.pallas_call(
        functools.partial(_q_prep_kernel, score_scale=score_scale),
        grid=(b, s // tm),
        in_specs=[
            pl.BlockSpec((1, tm, d), lambda i, j: (i, j, 0)),
            _resident((1, d)), _resident(w_in.shape), _resident((1, lora)),
            _resident((lora, MLA_HEADS * Q_GROUP)), _resident((1, QK_NOPE_DIM)),
            _resident((1, LANES)), _resident((1, LANES)), tab, tab,
            mem_spec, mem_spec, _resident((1, MEM_WIDTH)), _resident((MEM_WIDTH, MEM_WIDTH)),
        ],
        out_specs=[
            pl.BlockSpec((1, MLA_HEADS, tm, QK_PAD_DIM), lambda i, j: (i, 0, j, 0)),
            pl.BlockSpec((1, tm, MEM_WIDTH), lambda i, j: (i, j, 0)),
        ],
        out_shape=[
            jax.ShapeDtypeStruct((b, MLA_HEADS, s, QK_PAD_DIM), BF16),
            jax.ShapeDtypeStruct((b, s, MEM_WIDTH), BF16),
        ],
        compiler_params=_compiler_params(2),
        name="q_prep",
    )(x, g.reshape(1, d), w_in, g_q_lora.reshape(1, lora), wuq, g_q_nope.reshape(1, QK_NOPE_DIM),
      gqr, gqrs, cos, sin, kbd, vbd, gq, ones_bd)


def _flash_kernel(q_ref, k_ref, vt_ref, o_ref, s_ref, mcur_ref, m_ref, acc_ref):
    blk = q_ref.shape[2]
    qi = pl.program_id(2)
    q = q_ref[0, 0]

    def scores(j, diagonal):
        start = pl.multiple_of(j * blk, blk)
        s = _dot_nt(k_ref[0, 0, pl.ds(start, blk), :], q)
        if diagonal:
            key = lax.broadcasted_iota(jnp.int32, (blk, blk), 0)
            qry = lax.broadcasted_iota(jnp.int32, (blk, blk), 1)
            s = jnp.where(key <= qry, s, MASK_VALUE)
        s_ref[...] = s
        mcur_ref[...] = jnp.max(s, axis=0, keepdims=True)

    def absorb(j):
        m_prev = m_ref[...]
        m_new = jnp.maximum(m_prev, mcur_ref[...])
        alpha = jnp.exp2(m_prev - m_new)
        p = jnp.exp2(s_ref[...] - m_new).astype(BF16)
        acc_ref[...] = alpha * acc_ref[...] + _dot(vt_ref[0, 0, j], p)
        m_ref[...] = m_new

    m_ref[...] = jnp.full_like(m_ref, MASK_VALUE)
    acc_ref[...] = jnp.zeros_like(acc_ref)
    scores(qi, diagonal=True)

    def trip(j, carry):
        absorb(jnp.where(j == 0, qi, j - 1))
        scores(j, diagonal=False)
        return carry

    lax.fori_loop(0, qi, trip, 0)
    absorb(jnp.maximum(qi - 1, 0))
    acc = acc_ref[...]
    out_t = acc[:V_HEAD_DIM, :] / acc[V_HEAD_DIM:V_HEAD_DIM + 1, :]
    o_ref[0] = out_t.T.astype(BF16)


def _flash(q, k, vt, blk):
    b, heads, s, _ = q.shape
    return pl.pallas_call(
        _flash_kernel,
        grid=(b, heads, s // blk),
        in_specs=[
            pl.BlockSpec((1, 1, blk, QK_PAD_DIM), lambda i, h, j: (i, h, j, 0)),
            pl.BlockSpec((1, 1, s, QK_PAD_DIM), lambda i, h, j: (i, h, 0, 0)),
            pl.BlockSpec((1, 1, s // blk, VT_ROWS, blk), lambda i, h, j: (i, h, 0, 0, 0)),
        ],
        out_specs=pl.BlockSpec((1, blk, V_HEAD_DIM), lambda i, h, j: (i, j, h)),
        out_shape=jax.ShapeDtypeStruct((b, s, heads * V_HEAD_DIM), BF16),
        scratch_shapes=[
            pltpu.VMEM((blk, blk), F32), pltpu.VMEM((1, blk), F32), pltpu.VMEM((1, blk), F32),
            pltpu.VMEM((VT_ROWS, blk), F32),
        ],
        compiler_params=_compiler_params(3),
        name="flash",
    )(q, k, vt)


def kernel(x, mem, positions, norm_g, ffn_w13, ffn_w2, w_out, mem_norm_g, w_mem_kv, g_mem_q, g_mem_k,
           conv_w_in, conv_w, mla_w_in, g_q_lora, w_uq, g_q_nope, g_q_rope,
           kv_norm_g, w_dkv, g_ckv, w_ukv, w_kr, g_k_nope, g_k_rope):
    b, s, d = x.shape
    depth = norm_g.shape[0]
    n_conv = conv_w_in.shape[0]
    t = b * s
    tm = min(TOKEN_TILE, s)
    blk = min(FLASH_BLOCK, s)
    assert s % blk == 0 and blk % tm == 0 and d - conv_w.shape[2] == MEM_WIDTH

    head_of = jnp.arange(MEM_WIDTH, dtype=jnp.int32) // MEM_HEAD_DIM
    ones_bd = (head_of[:, None] == head_of[None, :]).astype(BF16)
    kbd, vbd = _mem_kv(mem, mem_norm_g, w_mem_kv, g_mem_k, ones_bd)
    gq_all = jnp.tile(g_mem_q, (1, MEM_HEADS)).reshape(depth, 1, MEM_WIDTH)
    cos, sin = _rope_tables(positions, tm)
    cos = cos.reshape(b, s, LANES)
    sin = sin.reshape(b, s, LANES)

    w13 = ffn_w13.astype(BF16)
    w2 = ffn_w2.astype(BF16)
    wo = w_out.astype(BF16)

    xf = x.reshape(t, d)
    shared = None
    for layer in range(depth):
        if layer == n_conv:
            shared = _kv_prep(xf.reshape(b, s, d), kv_norm_g, w_dkv, g_ckv, w_ukv, w_kr, g_k_nope,
                              g_k_rope, cos, sin, tm, blk)
        xf = _ffn(xf, norm_g[layer, 0], w13[layer, 0], w2[layer, 0], tm)
        x3 = xf.reshape(b, s, d)
        if layer < n_conv:
            ymix, ymem = _conv_mixer(x3, norm_g[layer, 1], conv_w_in[layer].astype(BF16), conv_w[layer],
                                     kbd, vbd, layer, gq_all[layer], ones_bd, tm)
        else:
            j = layer - n_conv
            q, ymem = _q_prep(x3, norm_g[layer, 1], mla_w_in[j].astype(BF16), g_q_lora[j], w_uq[j],
                              g_q_nope[j], g_q_rope[j], cos, sin, kbd, vbd, layer, gq_all[layer],
                              ones_bd, tm)
            ymix = _flash(q, shared[0], shared[1], blk)
        xf = _proj_ffn(xf, ymix.reshape(t, -1), ymem.reshape(t, MEM_WIDTH), wo[layer],
                       norm_g[layer, 2], w13[layer, 1], w2[layer, 1], tm)
    return xf.reshape(b, s, d)
```

```python
import functools
import math

import jax
import jax.numpy as jnp
from jax import lax
from jax.experimental import pallas as pl
from jax.experimental.pallas import tpu as pltpu

F32 = jnp.float32
BF16 = jnp.bfloat16

EPS = 1e-6
FFN_HALF = 0.5
MEM_HEADS = 4
MEM_HEAD_DIM = 64
MEM_WIDTH = MEM_HEADS * MEM_HEAD_DIM
CONV_WIDTH = 3
MLA_HEADS = 6
QK_NOPE_DIM = 128
QK_ROPE_DIM = 64
ROPE_HALF = QK_ROPE_DIM // 2
QK_HEAD_DIM = QK_NOPE_DIM + QK_ROPE_DIM
V_HEAD_DIM = 128
ROPE_THETA = 10000.0

LANES = 128
SUBLANES = 8
QK_PAD_DIM = 2 * LANES
Q_GROUP = 3 * LANES
BF16_SUBLANES = 2 * SUBLANES
VT_ROWS = V_HEAD_DIM + BF16_SUBLANES
V7X_VMEM_BYTES = 64 * 1024 * 1024
VMEM_LIMIT_BYTES = V7X_VMEM_BYTES - 8 * 1024 * 1024
MASK_VALUE = -1e30

TOKEN_TILE = 512
MIXER_TILE = 1024
SUB_TILES = 2
FLASH_BLOCK = 1024


def _compiler_params(n_axes):
    return pltpu.CompilerParams(
        dimension_semantics=("arbitrary",) * n_axes, vmem_limit_bytes=VMEM_LIMIT_BYTES)


def _resident(shape, lead=()):
    block = (None,) * len(lead) + tuple(shape[len(lead):])
    index = tuple(lead) + (0,) * (len(shape) - len(lead))
    return pl.BlockSpec(block, lambda *_: index, pipeline_mode=pl.Buffered(1))


def _dot(a, b):
    return jnp.dot(a, b, preferred_element_type=F32)


def _dot_nt(a, b):
    return lax.dot_general(a, b, (((1,), (1,)), ((), ())), preferred_element_type=F32)


def _rms(x, g):
    return x * lax.rsqrt(jnp.mean(x * x, axis=-1, keepdims=True) + EPS) * g


def _group_sum(sq, ones_bd):
    hi = sq.astype(BF16)
    lo = (sq - hi.astype(F32)).astype(BF16)
    return _dot(hi, ones_bd) + _dot(lo, ones_bd)


def _rope_tables_kernel(pos_ref, inv_freq_ref, cos_ref, sin_ref):
    ang = pos_ref[...].astype(F32) * inv_freq_ref[...]
    cos_ref[...] = jnp.cos(ang)
    sin_ref[...] = jnp.sin(ang)


def _rope_tables(positions, tm):
    t = positions.size
    pos = positions.reshape(t, 1)
    inv_freq = ROPE_THETA ** (-jnp.arange(0, QK_ROPE_DIM, 2, dtype=F32) / QK_ROPE_DIM)
    inv_freq = jnp.tile(inv_freq, LANES // ROPE_HALF).reshape(1, LANES)
    return pl.pallas_call(
        _rope_tables_kernel,
        grid=(t // tm,),
        in_specs=[pl.BlockSpec((tm, 1), lambda i: (i, 0)), _resident((1, LANES))],
        out_specs=[pl.BlockSpec((tm, LANES), lambda i: (i, 0))] * 2,
        out_shape=[jax.ShapeDtypeStruct((t, LANES), F32)] * 2,
        compiler_params=_compiler_params(1),
        name="rope_tables",
    )(pos, inv_freq)


def _mem_kv_kernel(mem_ref, g_ref, w_ref, gk_ref, ones_ref, kbd_ref, vbd_ref):
    m = mem_ref.shape[1]
    hm = _rms(mem_ref[0], g_ref[0]).astype(BF16)
    kv = _dot(hm, w_ref[0])
    k = kv[:, :MEM_WIDTH]
    v = kv[:, MEM_WIDTH:]
    ms = _group_sum(k * k, ones_ref[...]) * (1.0 / MEM_HEAD_DIM)
    kn = k * lax.rsqrt(ms + EPS) * gk_ref[0] * (MEM_HEAD_DIM ** -0.5)
    head_of_lane = lax.broadcasted_iota(jnp.int32, (m, MEM_WIDTH), 1) // MEM_HEAD_DIM
    for h in range(MEM_HEADS):
        sel = head_of_lane == h
        kbd_ref[0, 0, h * m:(h + 1) * m, :] = jnp.where(sel, kn, 0.0).astype(BF16)
        vbd_ref[0, 0, h * m:(h + 1) * m, :] = jnp.where(sel, v, 0.0).astype(BF16)


def _mem_kv(mem, mem_norm_g, w_mem_kv, g_mem_k, ones_bd):
    b, m, d = mem.shape
    depth = w_mem_kv.shape[0]
    gk = jnp.tile(g_mem_k, (1, MEM_HEADS)).reshape(depth, 1, MEM_WIDTH)
    out = jax.ShapeDtypeStruct((depth, b, MEM_HEADS * m, MEM_WIDTH), BF16)
    out_spec = pl.BlockSpec((1, 1, MEM_HEADS * m, MEM_WIDTH), lambda l, i: (l, i, 0, 0))
    return pl.pallas_call(
        _mem_kv_kernel,
        grid=(depth, b),
        in_specs=[
            pl.BlockSpec((1, m, d), lambda l, i: (i, 0, 0)),
            pl.BlockSpec((1, 1, d), lambda l, i: (l, 0, 0)),
            pl.BlockSpec((1, d, 2 * MEM_WIDTH), lambda l, i: (l, 0, 0)),
            pl.BlockSpec((1, 1, MEM_WIDTH), lambda l, i: (l, 0, 0)),
            _resident((MEM_WIDTH, MEM_WIDTH)),
        ],
        out_specs=[out_spec, out_spec],
        out_shape=[out, out],
        compiler_params=_compiler_params(2),
        name="mem_kv",
    )(mem, mem_norm_g.reshape(depth, 1, d), w_mem_kv.astype(BF16), gk, ones_bd)


def _memory_attention(q, kbd, vbd, gq, ones_bd):
    tm = q.shape[0]
    m = kbd.shape[0] // MEM_HEADS
    ms = _group_sum(q * q, ones_bd) * (1.0 / MEM_HEAD_DIM)
    qn = (q * lax.rsqrt(ms + EPS) * gq).astype(BF16)
    s = _dot_nt(qn, kbd)
    probs = []
    denoms = []
    for h in range(MEM_HEADS):
        sh = s[:, h * m:(h + 1) * m]
        p = jnp.exp(sh - jnp.max(sh, axis=-1, keepdims=True))
        denoms.append(jnp.sum(p, axis=-1, keepdims=True))
        probs.append(p.astype(BF16))
    o = _dot(jnp.concatenate(probs, axis=-1), vbd)
    head_of_lane = lax.broadcasted_iota(jnp.int32, (tm, MEM_WIDTH), 1) // MEM_HEAD_DIM
    denom = denoms[MEM_HEADS - 1]
    for h in range(MEM_HEADS - 2, -1, -1):
        denom = jnp.where(head_of_lane == h, denoms[h], denom)
    return o / denom


def _swiglu_step(x, g_ref, w13_ref, w2_ref):
    d_ff = w2_ref.shape[0]
    h = _rms(x, g_ref[...]).astype(BF16)
    gu = _dot(h, w13_ref[...])
    gate = gu[:, :d_ff]
    up = gu[:, d_ff:]
    act = (gate * jax.nn.sigmoid(gate) * up).astype(BF16)
    return x + FFN_HALF * _dot(act, w2_ref[...])


def _ffn_kernel(x_ref, g_ref, w13_ref, w2_ref, o_ref):
    o_ref[...] = _swiglu_step(x_ref[...], g_ref, w13_ref, w2_ref)


def _proj_ffn_kernel(x_ref, ymix_ref, ymem_ref, wout_ref, g_ref, w13_ref, w2_ref, o_ref):
    mix = ymix_ref.shape[1]
    x1 = (x_ref[...] + _dot(ymix_ref[...], wout_ref[:mix, :])
          + _dot(ymem_ref[...], wout_ref[mix:, :]))
    o_ref[...] = _swiglu_step(x1, g_ref, w13_ref, w2_ref)


def _ffn(x, norm_g, w13, w2, layer, tm):
    t, d = x.shape
    row = pl.BlockSpec((tm, d), lambda i: (i, 0))
    return pl.pallas_call(
        _ffn_kernel,
        grid=(t // tm,),
        in_specs=[row, _resident(norm_g.shape, (layer, 0)), _resident(w13.shape, (layer, 0)),
                  _resident(w2.shape, (layer, 0))],
        out_specs=row,
        out_shape=jax.ShapeDtypeStruct((t, d), F32),
        compiler_params=_compiler_params(1),
        name="ffn",
    )(x, norm_g, w13, w2)


def _proj_ffn(x, ymix, ymem, w_out, norm_g, w13, w2, layer, tm):
    t, d = x.shape
    mix = ymix.shape[1]
    row = pl.BlockSpec((tm, d), lambda i: (i, 0))
    return pl.pallas_call(
        _proj_ffn_kernel,
        grid=(t // tm,),
        in_specs=[
            row,
            pl.BlockSpec((tm, mix), lambda i: (i, 0)),
            pl.BlockSpec((tm, d - mix), lambda i: (i, 0)),
            _resident(w_out.shape, (layer,)), _resident(norm_g.shape, (layer, 2)),
            _resident(w13.shape, (layer, 1)), _resident(w2.shape, (layer, 1)),
        ],
        out_specs=row,
        out_shape=jax.ShapeDtypeStruct((t, d), F32),
        compiler_params=_compiler_params(1),
        name="proj_ffn",
    )(x, ymix, ymem, w_out, norm_g, w13, w2)


def _conv_mixer_kernel(x_ref, g_ref, win_ref, convw_ref, kbd_ref, vbd_ref, gq_ref, ones_ref,
                       ymix_ref, ymem_ref, carry_ref):
    tm = x_ref.shape[1]
    ch = ymix_ref.shape[2]

    @pl.when(pl.program_id(1) == 0)
    def _():
        carry_ref[...] = jnp.zeros_like(carry_ref)

    sub = tm // SUB_TILES
    w = convw_ref[...]
    row = lax.broadcasted_iota(jnp.int32, (sub, ch), 0)
    tail = carry_ref[...]
    for r in range(SUB_TILES):
        rs = slice(r * sub, (r + 1) * sub)
        h = _rms(x_ref[0, rs, :], g_ref[...]).astype(BF16)
        q_mem = _dot(h, win_ref[:, 3 * ch:])
        ymem_ref[0, rs, :] = _memory_attention(
            q_mem, kbd_ref[0, 0], vbd_ref[0, 0], gq_ref[...], ones_ref[...]).astype(BF16)
        proj = _dot(h, win_ref[:, :3 * ch])
        gate_b = proj[:, :ch]
        u = proj[:, ch:2 * ch] * proj[:, 2 * ch:3 * ch]
        prev2 = tail[SUBLANES - 2:SUBLANES - 1, :]
        prev1 = tail[SUBLANES - 1:SUBLANES, :]
        u1 = jnp.where(row == 0, prev1, pltpu.roll(u, 1, 0))
        u2 = jnp.where(row == 0, prev2, jnp.where(row == 1, prev1, pltpu.roll(u, 2, 0)))
        tail = u[sub - SUBLANES:, :]
        conv = w[0:1, :] * u2 + w[1:2, :] * u1 + w[2:3, :] * u
        ymix_ref[0, rs, :] = (gate_b * conv).astype(BF16)
    carry_ref[...] = tail


def _conv_mixer(x, norm_g, w_in, conv_w, kbd, vbd, gq, ones_bd, layer, tm):
    b, s, d = x.shape
    ch = conv_w.shape[2]
    mem_rows = kbd.shape[2]
    mem_spec = pl.BlockSpec((1, 1, mem_rows, MEM_WIDTH), lambda i, j: (layer, i, 0, 0))
    return pl.pallas_call(
        _conv_mixer_kernel,
        grid=(b, s // tm),
        in_specs=[
            pl.BlockSpec((1, tm, d), lambda i, j: (i, j, 0)),
            _resident(norm_g.shape, (layer, 1)), _resident(w_in.shape, (layer,)),
            _resident(conv_w.shape, (layer,)),
            mem_spec, mem_spec, _resident(gq.shape, (layer,)), _resident((MEM_WIDTH, MEM_WIDTH)),
        ],
        out_specs=[
            pl.BlockSpec((1, tm, ch), lambda i, j: (i, j, 0)),
            pl.BlockSpec((1, tm, MEM_WIDTH), lambda i, j: (i, j, 0)),
        ],
        out_shape=[
            jax.ShapeDtypeStruct((b, s, ch), BF16),
            jax.ShapeDtypeStruct((b, s, MEM_WIDTH), BF16),
        ],
        scratch_shapes=[pltpu.VMEM((SUBLANES, ch), F32)],
        compiler_params=_compiler_params(2),
        name="conv_mixer",
    )(x, norm_g, w_in, conv_w, kbd, vbd, gq, ones_bd)


def _rope_rows(r, r_swapped, g, g_swapped, cos, sin):
    inv = lax.rsqrt(jnp.sum(r * r, axis=-1, keepdims=True) * (1.0 / QK_ROPE_DIM) + EPS)
    return (r * g * cos + r_swapped * g_swapped * sin) * inv


def _kv_prep_kernel(x_ref, g_ref, wd_ref, gckv_ref, wukv_ref, gkn_ref, gkr_ref, gkrs_ref,
                    cos_ref, sin_ref, k_ref, vt_ref):
    tm = x_ref.shape[1]
    lora = gckv_ref.shape[1]
    ones_rows = jnp.where(
        lax.broadcasted_iota(jnp.int32, (VT_ROWS - V_HEAD_DIM, tm), 0) == 0, 1.0, 0.0)
    h = _rms(x_ref[0], g_ref[...]).astype(BF16)
    d = _dot(h, wd_ref[...])
    ckv = _rms(d[:, :lora], gckv_ref[...]).astype(BF16)
    kr = _rope_rows(d[:, lora:lora + LANES], d[:, lora + LANES:], gkr_ref[...], gkrs_ref[...],
                    cos_ref[0], sin_ref[0]).astype(BF16)
    kv = _dot(ckv, wukv_ref[...])
    for hd in range(MLA_HEADS):
        base = hd * (QK_NOPE_DIM + V_HEAD_DIM)
        kn = _rms(kv[:, base:base + QK_NOPE_DIM], gkn_ref[...])
        k_ref[0, hd, :, :QK_NOPE_DIM] = kn.astype(BF16)
        k_ref[0, hd, :, QK_NOPE_DIM:] = kr
        v = kv[:, base + QK_NOPE_DIM:base + QK_NOPE_DIM + V_HEAD_DIM]
        vt_ref[0, hd, 0] = jnp.concatenate([v.T, ones_rows], axis=0).astype(BF16)


def _swap_rope_cols(w):
    pad = jnp.zeros(w.shape[:-1] + (LANES - QK_ROPE_DIM,), w.dtype)
    x1, x2 = w[..., :ROPE_HALF], w[..., ROPE_HALF:]
    return jnp.concatenate([w, pad], axis=-1), jnp.concatenate([-x2, x1, pad], axis=-1)


def _swap_rope_gain(g):
    pad = jnp.zeros((LANES - QK_ROPE_DIM,), g.dtype)
    straight = jnp.concatenate([g, pad])
    swapped = jnp.concatenate([g[ROPE_HALF:], g[:ROPE_HALF], pad])
    return straight.reshape(1, LANES), swapped.reshape(1, LANES)


def _kv_prep(x, kv_norm_g, w_dkv, g_ckv, w_ukv, w_kr, g_k_nope, g_k_rope, cos, sin, tm, blk):
    b, s, d = x.shape
    lora = w_dkv.shape[1]
    per_blk = blk // tm
    kr_cols, kr_swapped = _swap_rope_cols(w_kr)
    wd = jnp.concatenate([w_dkv, kr_cols, kr_swapped], axis=-1).astype(BF16)
    gkr, gkrs = _swap_rope_gain(g_k_rope)
    tab = pl.BlockSpec((1, tm, LANES), lambda i, j: (i, j, 0))
    return pl.pallas_call(
        _kv_prep_kernel,
        grid=(b, s // tm),
        in_specs=[
            pl.BlockSpec((1, tm, d), lambda i, j: (i, j, 0)),
            _resident((1, d)), _resident((d, lora + 2 * LANES)), _resident((1, lora)),
            _resident(w_ukv.shape), _resident((1, QK_NOPE_DIM)),
            _resident((1, LANES)), _resident((1, LANES)), tab, tab,
        ],
        out_specs=[
            pl.BlockSpec((1, MLA_HEADS, tm, QK_PAD_DIM), lambda i, j: (i, 0, j, 0)),
            pl.BlockSpec((1, MLA_HEADS, 1, VT_ROWS, tm),
                         lambda i, j: (i, 0, j // per_blk, 0, j % per_blk)),
        ],
        out_shape=[
            jax.ShapeDtypeStruct((b, MLA_HEADS, s, QK_PAD_DIM), BF16),
            jax.ShapeDtypeStruct((b, MLA_HEADS, s // blk, VT_ROWS, blk), BF16),
        ],
        compiler_params=_compiler_params(2),
        name="kv_prep",
    )(x, kv_norm_g.reshape(1, d), wd, g_ckv.reshape(1, lora), w_ukv.astype(BF16),
      g_k_nope.reshape(1, QK_NOPE_DIM), gkr, gkrs, cos, sin)


def _q_prep_kernel(x_ref, g_ref, win_ref, gql_ref, wuq_ref, gqn_ref, gqr_ref, gqrs_ref,
                   cos_ref, sin_ref, kbd_ref, vbd_ref, gq_ref, ones_ref, q_ref, ymem_ref,
                   *, score_scale):
    lora = gql_ref.shape[1]
    sub = x_ref.shape[1] // SUB_TILES
    for r in range(SUB_TILES):
        rs = slice(r * sub, (r + 1) * sub)
        h = _rms(x_ref[0, rs, :], g_ref[...]).astype(BF16)
        proj = _dot(h, win_ref[...])
        cq = _rms(proj[:, :lora], gql_ref[...]).astype(BF16)
        qall = _dot(cq, wuq_ref[...])
        cos = cos_ref[0, rs, :]
        sin = sin_ref[0, rs, :]
        for hd in range(MLA_HEADS):
            base = hd * Q_GROUP
            qn = _rms(qall[:, base:base + LANES], gqn_ref[...])
            qr = _rope_rows(qall[:, base + LANES:base + 2 * LANES],
                            qall[:, base + 2 * LANES:base + Q_GROUP],
                            gqr_ref[...], gqrs_ref[...], cos, sin)
            q_ref[0, hd, rs, :QK_NOPE_DIM] = (qn * score_scale).astype(BF16)
            q_ref[0, hd, rs, QK_NOPE_DIM:] = (qr * score_scale).astype(BF16)
        ymem_ref[0, rs, :] = _memory_attention(
            proj[:, lora:], kbd_ref[0, 0], vbd_ref[0, 0], gq_ref[...], ones_ref[...]).astype(BF16)


def _q_prep(x, norm_g, w_in, g_q_lora, w_uq, g_q_nope, g_q_rope, cos, sin, kbd, vbd, gq, ones_bd,
            layer, mla_layer, tm):
    b, s, d = x.shape
    lora = g_q_lora.shape[0]
    w = w_uq.reshape(lora, MLA_HEADS, QK_HEAD_DIM)
    rope_cols, rope_swapped = _swap_rope_cols(w[..., QK_NOPE_DIM:])
    wuq = jnp.concatenate([w[..., :QK_NOPE_DIM], rope_cols, rope_swapped], axis=-1)
    wuq = wuq.reshape(lora, MLA_HEADS * Q_GROUP).astype(BF16)
    gqr, gqrs = _swap_rope_gain(g_q_rope)
    mem_rows = kbd.shape[2]
    mem_spec = pl.BlockSpec((1, 1, mem_rows, MEM_WIDTH), lambda i, j: (layer, i, 0, 0))
    tab = pl.BlockSpec((1, tm, LANES), lambda i, j: (i, j, 0))
    score_scale = QK_HEAD_DIM ** -0.5 * math.log2(math.e)
    return pl.pallas_call(
        functools.partial(_q_prep_kernel, score_scale=score_scale),
        grid=(b, s // tm),
        in_specs=[
            pl.BlockSpec((1, tm, d), lambda i, j: (i, j, 0)),
            _resident(norm_g.shape, (layer, 1)), _resident(w_in.shape, (mla_layer,)),
            _resident((1, lora)),
            _resident((lora, MLA_HEADS * Q_GROUP)), _resident((1, QK_NOPE_DIM)),
            _resident((1, LANES)), _resident((1, LANES)), tab, tab,
            mem_spec, mem_spec, _resident(gq.shape, (layer,)), _resident((MEM_WIDTH, MEM_WIDTH)),
        ],
        out_specs=[
            pl.BlockSpec((1, MLA_HEADS, tm, QK_PAD_DIM), lambda i, j: (i, 0, j, 0)),
            pl.BlockSpec((1, tm, MEM_WIDTH), lambda i, j: (i, j, 0)),
        ],
        out_shape=[
            jax.ShapeDtypeStruct((b, MLA_HEADS, s, QK_PAD_DIM), BF16),
            jax.ShapeDtypeStruct((b, s, MEM_WIDTH), BF16),
        ],
        compiler_params=_compiler_params(2),
        name="q_prep",
    )(x, norm_g, w_in, g_q_lora.reshape(1, lora), wuq, g_q_nope.reshape(1, QK_NOPE_DIM),
      gqr, gqrs, cos, sin, kbd, vbd, gq, ones_bd)


def _flash_items(nq):
    items = [(i, i) for i in range(nq)]
    items += [(i, j) for j in range(nq - 1) for i in range(j + 1, nq)]
    return items


def _emit_steps(first, count, step):
    def pair(u, carry):
        t = first + 2 * u
        step(t, first % 2)
        step(t + 1, (first + 1) % 2)
        return carry

    lax.fori_loop(0, count // 2, pair, 0)
    if count % 2:
        last = first + count - 1
        step(last, last % 2)


def _flash_kernel(qidx_ref, kidx_ref, q_ref, k_ref, vt_ref, o_ref,
                  s0_ref, s1_ref, mc0_ref, mc1_ref, m_ref, acc_ref, *, n_items):
    blk = s0_ref.shape[0]
    nq = m_ref.shape[0]
    s_bufs = (s0_ref, s1_ref)
    mc_bufs = (mc0_ref, mc1_ref)

    def rows(idx):
        return pl.ds(pl.multiple_of(idx * blk, blk), blk)

    def scores(t, slot, diagonal):
        s = _dot_nt(k_ref[0, 0, rows(kidx_ref[t]), :], q_ref[0, 0, rows(qidx_ref[t]), :])
        if diagonal:
            key = lax.broadcasted_iota(jnp.int32, (blk, blk), 0)
            qry = lax.broadcasted_iota(jnp.int32, (blk, blk), 1)
            s = jnp.where(key <= qry, s, MASK_VALUE)
        s_bufs[slot][...] = s
        mc_bufs[slot][...] = jnp.max(s, axis=0, keepdims=True)

    def absorb(t, slot, diagonal):
        qi = qidx_ref[t]
        m_new = mc_bufs[slot][...]
        if not diagonal:
            m_prev = m_ref[qi]
            m_new = jnp.maximum(m_prev, m_new)
            alpha = jnp.exp2(m_prev - m_new)
        p = jnp.exp2(s_bufs[slot][...] - m_new).astype(BF16)
        pv = _dot(vt_ref[0, 0, kidx_ref[t]], p)
        acc_ref[qi] = pv if diagonal else alpha * acc_ref[qi] + pv
        m_ref[qi] = m_new

    def step(absorb_diagonal, scores_diagonal):
        def run(t, slot):
            scores(t + 1, 1 - slot, scores_diagonal)
            absorb(t, slot, absorb_diagonal)
        return run

    scores(0, 0, True)
    _emit_steps(0, nq - 1, step(True, True))
    if n_items > nq:
        _emit_steps(nq - 1, 1, step(True, False))
        _emit_steps(nq, n_items - 1 - nq, step(False, False))
    absorb(n_items - 1, (n_items - 1) % 2, n_items == nq)

    def finish(qi, carry):
        acc = acc_ref[qi]
        out_t = acc[:V_HEAD_DIM, :] / acc[V_HEAD_DIM:V_HEAD_DIM + 1, :]
        o_ref[0, rows(qi), :] = out_t.T.astype(BF16)
        return carry

    lax.fori_loop(0, nq, finish, 0)


def _flash(q, k, vt, blk):
    b, heads, s, _ = q.shape
    nq = s // blk
    items = _flash_items(nq)
    qidx = jnp.asarray([i for i, _ in items], jnp.int32)
    kidx = jnp.asarray([j for _, j in items], jnp.int32)
    grid_spec = pltpu.PrefetchScalarGridSpec(
        num_scalar_prefetch=2,
        grid=(b, heads),
        in_specs=[
            pl.BlockSpec((1, 1, s, QK_PAD_DIM), lambda i, h, *_: (i, h, 0, 0)),
            pl.BlockSpec((1, 1, s, QK_PAD_DIM), lambda i, h, *_: (i, h, 0, 0)),
            pl.BlockSpec((1, 1, nq, VT_ROWS, blk), lambda i, h, *_: (i, h, 0, 0, 0)),
        ],
        out_specs=pl.BlockSpec((1, s, V_HEAD_DIM), lambda i, h, *_: (i, 0, h)),
        scratch_shapes=[
            pltpu.VMEM((blk, blk), F32), pltpu.VMEM((blk, blk), F32),
            pltpu.VMEM((1, blk), F32), pltpu.VMEM((1, blk), F32),
            pltpu.VMEM((nq, 1, blk), F32), pltpu.VMEM((nq, VT_ROWS, blk), F32),
        ],
    )
    return pl.pallas_call(
        functools.partial(_flash_kernel, n_items=len(items)),
        grid_spec=grid_spec,
        out_shape=jax.ShapeDtypeStruct((b, s, heads * V_HEAD_DIM), BF16),
        compiler_params=_compiler_params(2),
        name="flash",
    )(qidx, kidx, q, k, vt)


def kernel(x, mem, positions, norm_g, ffn_w13, ffn_w2, w_out, mem_norm_g, w_mem_kv, g_mem_q, g_mem_k,
           conv_w_in, conv_w, mla_w_in, g_q_lora, w_uq, g_q_nope, g_q_rope,
           kv_norm_g, w_dkv, g_ckv, w_ukv, w_kr, g_k_nope, g_k_rope):
    b, s, d = x.shape
    depth = norm_g.shape[0]
    n_conv = conv_w_in.shape[0]
    t = b * s
    tm = min(TOKEN_TILE, s)
    blk = min(FLASH_BLOCK, s)
    assert s % blk == 0 and blk % tm == 0 and d - conv_w.shape[2] == MEM_WIDTH

    head_of = jnp.arange(MEM_WIDTH, dtype=jnp.int32) // MEM_HEAD_DIM
    ones_bd = (head_of[:, None] == head_of[None, :]).astype(BF16)
    kbd, vbd = _mem_kv(mem, mem_norm_g, w_mem_kv, g_mem_k, ones_bd)
    gq_all = jnp.tile(g_mem_q, (1, MEM_HEADS)).reshape(depth, 1, MEM_WIDTH)
    cos, sin = _rope_tables(positions, tm)
    cos = cos.reshape(b, s, LANES)
    sin = sin.reshape(b, s, LANES)

    w13 = ffn_w13.astype(BF16)
    w2 = ffn_w2.astype(BF16)
    wo = w_out.astype(BF16)
    conv_in = conv_w_in.astype(BF16)
    mla_in = mla_w_in.astype(BF16)
    gn = norm_g.reshape(depth, norm_g.shape[1], 1, d)

    xf = x.reshape(t, d)
    shared = None
    for layer in range(depth):
        if layer == n_conv:
            shared = _kv_prep(xf.reshape(b, s, d), kv_norm_g, w_dkv, g_ckv, w_ukv, w_kr, g_k_nope,
                              g_k_rope, cos, sin, tm, blk)
        xf = _ffn(xf, gn, w13, w2, layer, tm)
        x3 = xf.reshape(b, s, d)
        if layer < n_conv:
            ymix, ymem = _conv_mixer(x3, gn, conv_in, conv_w, kbd, vbd, gq_all, ones_bd, layer,
                                     min(MIXER_TILE, s))
        else:
            j = layer - n_conv
            q, ymem = _q_prep(x3, gn, mla_in, g_q_lora[j], w_uq[j], g_q_nope[j], g_q_rope[j],
                              cos, sin, kbd, vbd, gq_all, ones_bd, layer, j, min(MIXER_TILE, s))
            ymix = _flash(q, shared[0], shared[1], blk)
        xf = _proj_ffn(xf, ymix.reshape(t, -1), ymem.reshape(t, MEM_WIDTH), wo, gn, w13, w2,
                       layer, tm)
    return xf.reshape(b, s, d)
```

```python
import functools
import math

import jax
import jax.numpy as jnp
from jax import lax
from jax.experimental import pallas as pl
from jax.experimental.pallas import tpu as pltpu

F32 = jnp.float32
BF16 = jnp.bfloat16

EPS = 1e-6
FFN_HALF = 0.5
MEM_HEADS = 4
MEM_HEAD_DIM = 64
MEM_WIDTH = MEM_HEADS * MEM_HEAD_DIM
CONV_WIDTH = 3
MLA_HEADS = 6
QK_NOPE_DIM = 128
QK_ROPE_DIM = 64
ROPE_HALF = QK_ROPE_DIM // 2
QK_HEAD_DIM = QK_NOPE_DIM + QK_ROPE_DIM
V_HEAD_DIM = 128
ROPE_THETA = 10000.0

LANES = 128
SUBLANES = 8
QK_PAD_DIM = 2 * LANES
Q_GROUP = 3 * LANES
BF16_SUBLANES = 2 * SUBLANES
VT_ROWS = V_HEAD_DIM + BF16_SUBLANES
V7X_VMEM_BYTES = 64 * 1024 * 1024
VMEM_LIMIT_BYTES = V7X_VMEM_BYTES - 8 * 1024 * 1024
MASK_VALUE = -1e30

TOKEN_TILE = 512
MIXER_TILE = 1024
SUB_TILES = 2
FFN_TILE = 1024
FFN_SUB_ROWS = 256
FLASH_BLOCK = 1024
FLASH_UNROLL = 4


def _compiler_params(n_axes):
    return pltpu.CompilerParams(
        dimension_semantics=("arbitrary",) * n_axes, vmem_limit_bytes=VMEM_LIMIT_BYTES)


def _resident(shape, lead=()):
    block = (None,) * len(lead) + tuple(shape[len(lead):])
    index = tuple(lead) + (0,) * (len(shape) - len(lead))
    return pl.BlockSpec(block, lambda *_: index, pipeline_mode=pl.Buffered(1))


def _dot(a, b):
    return jnp.dot(a, b, preferred_element_type=F32)


def _dot_nt(a, b):
    return lax.dot_general(a, b, (((1,), (1,)), ((), ())), preferred_element_type=F32)


def _rms(x, g):
    return x * lax.rsqrt(jnp.mean(x * x, axis=-1, keepdims=True) + EPS) * g


def _group_sum(sq, ones_bd):
    hi = sq.astype(BF16)
    lo = (sq - hi.astype(F32)).astype(BF16)
    return _dot(hi, ones_bd) + _dot(lo, ones_bd)


def _rope_tables_kernel(pos_ref, inv_freq_ref, cos_ref, sin_ref):
    ang = pos_ref[...].astype(F32) * inv_freq_ref[...]
    cos_ref[...] = jnp.cos(ang)
    sin_ref[...] = jnp.sin(ang)


def _rope_tables(positions, tm):
    t = positions.size
    pos = positions.reshape(t, 1)
    inv_freq = ROPE_THETA ** (-jnp.arange(0, QK_ROPE_DIM, 2, dtype=F32) / QK_ROPE_DIM)
    inv_freq = jnp.tile(inv_freq, LANES // ROPE_HALF).reshape(1, LANES)
    return pl.pallas_call(
        _rope_tables_kernel,
        grid=(t // tm,),
        in_specs=[pl.BlockSpec((tm, 1), lambda i: (i, 0)), _resident((1, LANES))],
        out_specs=[pl.BlockSpec((tm, LANES), lambda i: (i, 0))] * 2,
        out_shape=[jax.ShapeDtypeStruct((t, LANES), F32)] * 2,
        compiler_params=_compiler_params(1),
        name="rope_tables",
    )(pos, inv_freq)


def _mem_kv_kernel(mem_ref, g_ref, w_ref, gk_ref, ones_ref, kbd_ref, vbd_ref):
    m = mem_ref.shape[1]
    hm = _rms(mem_ref[0], g_ref[0]).astype(BF16)
    kv = _dot(hm, w_ref[0])
    k = kv[:, :MEM_WIDTH]
    v = kv[:, MEM_WIDTH:]
    ms = _group_sum(k * k, ones_ref[...]) * (1.0 / MEM_HEAD_DIM)
    kn = k * lax.rsqrt(ms + EPS) * gk_ref[0] * (MEM_HEAD_DIM ** -0.5)
    head_of_lane = lax.broadcasted_iota(jnp.int32, (m, MEM_WIDTH), 1) // MEM_HEAD_DIM
    for h in range(MEM_HEADS):
        sel = head_of_lane == h
        kbd_ref[0, 0, h * m:(h + 1) * m, :] = jnp.where(sel, kn, 0.0).astype(BF16)
        vbd_ref[0, 0, h * m:(h + 1) * m, :] = jnp.where(sel, v, 0.0).astype(BF16)


def _mem_kv(mem, mem_norm_g, w_mem_kv, g_mem_k, ones_bd):
    b, m, d = mem.shape
    depth = w_mem_kv.shape[0]
    gk = jnp.tile(g_mem_k, (1, MEM_HEADS)).reshape(depth, 1, MEM_WIDTH)
    out = jax.ShapeDtypeStruct((depth, b, MEM_HEADS * m, MEM_WIDTH), BF16)
    out_spec = pl.BlockSpec((1, 1, MEM_HEADS * m, MEM_WIDTH), lambda l, i: (l, i, 0, 0))
    return pl.pallas_call(
        _mem_kv_kernel,
        grid=(depth, b),
        in_specs=[
            pl.BlockSpec((1, m, d), lambda l, i: (i, 0, 0)),
            pl.BlockSpec((1, 1, d), lambda l, i: (l, 0, 0)),
            pl.BlockSpec((1, d, 2 * MEM_WIDTH), lambda l, i: (l, 0, 0)),
            pl.BlockSpec((1, 1, MEM_WIDTH), lambda l, i: (l, 0, 0)),
            _resident((MEM_WIDTH, MEM_WIDTH)),
        ],
        out_specs=[out_spec, out_spec],
        out_shape=[out, out],
        compiler_params=_compiler_params(2),
        name="mem_kv",
    )(mem, mem_norm_g.reshape(depth, 1, d), w_mem_kv.astype(BF16), gk, ones_bd)


def _memory_attention(q, kbd, vbd, gq, ones_bd):
    tm = q.shape[0]
    m = kbd.shape[0] // MEM_HEADS
    ms = _group_sum(q * q, ones_bd) * (1.0 / MEM_HEAD_DIM)
    qn = (q * lax.rsqrt(ms + EPS) * gq).astype(BF16)
    s = _dot_nt(qn, kbd)
    probs = []
    denoms = []
    for h in range(MEM_HEADS):
        sh = s[:, h * m:(h + 1) * m]
        p = jnp.exp(sh - jnp.max(sh, axis=-1, keepdims=True))
        denoms.append(jnp.sum(p, axis=-1, keepdims=True))
        probs.append(p.astype(BF16))
    o = _dot(jnp.concatenate(probs, axis=-1), vbd)
    head_of_lane = lax.broadcasted_iota(jnp.int32, (tm, MEM_WIDTH), 1) // MEM_HEAD_DIM
    denom = denoms[MEM_HEADS - 1]
    for h in range(MEM_HEADS - 2, -1, -1):
        denom = jnp.where(head_of_lane == h, denoms[h], denom)
    return o / denom


def _swiglu_step(x, g_ref, w13_ref, w2_ref):
    d_ff = w2_ref.shape[0]
    h = _rms(x, g_ref[...]).astype(BF16)
    gu = _dot(h, w13_ref[...])
    gate = gu[:, :d_ff]
    up = gu[:, d_ff:]
    act = (gate * jax.nn.sigmoid(gate) * up).astype(BF16)
    return x + FFN_HALF * _dot(act, w2_ref[...])


def _ffn_kernel(x_ref, g_ref, w13_ref, w2_ref, o_ref):
    for r in range(x_ref.shape[0] // FFN_SUB_ROWS):
        rs = slice(r * FFN_SUB_ROWS, (r + 1) * FFN_SUB_ROWS)
        o_ref[rs, :] = _swiglu_step(x_ref[rs, :], g_ref, w13_ref, w2_ref)


def _proj_ffn_kernel(x_ref, ymix_ref, ymem_ref, wout_ref, g_ref, w13_ref, w2_ref, o_ref):
    mix = ymix_ref.shape[1]
    for r in range(x_ref.shape[0] // FFN_SUB_ROWS):
        rs = slice(r * FFN_SUB_ROWS, (r + 1) * FFN_SUB_ROWS)
        x1 = (x_ref[rs, :] + _dot(ymix_ref[rs, :], wout_ref[:mix, :])
              + _dot(ymem_ref[rs, :], wout_ref[mix:, :]))
        o_ref[rs, :] = _swiglu_step(x1, g_ref, w13_ref, w2_ref)


def _ffn(x, norm_g, w13, w2, layer, tm):
    t, d = x.shape
    row = pl.BlockSpec((tm, d), lambda i: (i, 0))
    return pl.pallas_call(
        _ffn_kernel,
        grid=(t // tm,),
        in_specs=[row, _resident(norm_g.shape, (layer, 0)), _resident(w13.shape, (layer, 0)),
                  _resident(w2.shape, (layer, 0))],
        out_specs=row,
        out_shape=jax.ShapeDtypeStruct((t, d), F32),
        compiler_params=_compiler_params(1),
        name="ffn",
    )(x, norm_g, w13, w2)


def _proj_ffn(x, ymix, ymem, w_out, norm_g, w13, w2, layer, tm):
    t, d = x.shape
    mix = ymix.shape[1]
    row = pl.BlockSpec((tm, d), lambda i: (i, 0))
    return pl.pallas_call(
        _proj_ffn_kernel,
        grid=(t // tm,),
        in_specs=[
            row,
            pl.BlockSpec((tm, mix), lambda i: (i, 0)),
            pl.BlockSpec((tm, d - mix), lambda i: (i, 0)),
            _resident(w_out.shape, (layer,)), _resident(norm_g.shape, (layer, 2)),
            _resident(w13.shape, (layer, 1)), _resident(w2.shape, (layer, 1)),
        ],
        out_specs=row,
        out_shape=jax.ShapeDtypeStruct((t, d), F32),
        compiler_params=_compiler_params(1),
        name="proj_ffn",
    )(x, ymix, ymem, w_out, norm_g, w13, w2)


def _conv_mixer_kernel(x_ref, g_ref, win_ref, convw_ref, kbd_ref, vbd_ref, gq_ref, ones_ref,
                       ymix_ref, ymem_ref, carry_ref):
    tm = x_ref.shape[1]
    ch = ymix_ref.shape[2]

    @pl.when(pl.program_id(1) == 0)
    def _():
        carry_ref[...] = jnp.zeros_like(carry_ref)

    sub = tm // SUB_TILES
    w = convw_ref[...]
    row = lax.broadcasted_iota(jnp.int32, (sub, ch), 0)
    tail = carry_ref[...]
    for r in range(SUB_TILES):
        rs = slice(r * sub, (r + 1) * sub)
        h = _rms(x_ref[0, rs, :], g_ref[...]).astype(BF16)
        q_mem = _dot(h, win_ref[:, 3 * ch:])
        ymem_ref[0, rs, :] = _memory_attention(
            q_mem, kbd_ref[0, 0], vbd_ref[0, 0], gq_ref[...], ones_ref[...]).astype(BF16)
        proj = _dot(h, win_ref[:, :3 * ch])
        gate_b = proj[:, :ch]
        u = proj[:, ch:2 * ch] * proj[:, 2 * ch:3 * ch]
        prev2 = tail[SUBLANES - 2:SUBLANES - 1, :]
        prev1 = tail[SUBLANES - 1:SUBLANES, :]
        u1 = jnp.where(row == 0, prev1, pltpu.roll(u, 1, 0))
        u2 = jnp.where(row == 0, prev2, jnp.where(row == 1, prev1, pltpu.roll(u, 2, 0)))
        tail = u[sub - SUBLANES:, :]
        conv = w[0:1, :] * u2 + w[1:2, :] * u1 + w[2:3, :] * u
        ymix_ref[0, rs, :] = (gate_b * conv).astype(BF16)
    carry_ref[...] = tail


def _conv_mixer(x, norm_g, w_in, conv_w, kbd, vbd, gq, ones_bd, layer, tm):
    b, s, d = x.shape
    ch = conv_w.shape[2]
    mem_rows = kbd.shape[2]
    mem_spec = pl.BlockSpec((1, 1, mem_rows, MEM_WIDTH), lambda i, j: (layer, i, 0, 0))
    return pl.pallas_call(
        _conv_mixer_kernel,
        grid=(b, s // tm),
        in_specs=[
            pl.BlockSpec((1, tm, d), lambda i, j: (i, j, 0)),
            _resident(norm_g.shape, (layer, 1)), _resident(w_in.shape, (layer,)),
            _resident(conv_w.shape, (layer,)),
            mem_spec, mem_spec, _resident(gq.shape, (layer,)), _resident((MEM_WIDTH, MEM_WIDTH)),
        ],
        out_specs=[
            pl.BlockSpec((1, tm, ch), lambda i, j: (i, j, 0)),
            pl.BlockSpec((1, tm, MEM_WIDTH), lambda i, j: (i, j, 0)),
        ],
        out_shape=[
            jax.ShapeDtypeStruct((b, s, ch), BF16),
            jax.ShapeDtypeStruct((b, s, MEM_WIDTH), BF16),
        ],
        scratch_shapes=[pltpu.VMEM((SUBLANES, ch), F32)],
        compiler_params=_compiler_params(2),
        name="conv_mixer",
    )(x, norm_g, w_in, conv_w, kbd, vbd, gq, ones_bd)


def _rope_rows(r, r_swapped, g, g_swapped, cos, sin):
    inv = lax.rsqrt(jnp.sum(r * r, axis=-1, keepdims=True) * (1.0 / QK_ROPE_DIM) + EPS)
    return (r * g * cos + r_swapped * g_swapped * sin) * inv


def _kv_prep_kernel(x_ref, g_ref, wd_ref, gckv_ref, wukv_ref, gkn_ref, gkr_ref, gkrs_ref,
                    cos_ref, sin_ref, k_ref, vt_ref):
    sub = x_ref.shape[1] // SUB_TILES
    lora = gckv_ref.shape[1]
    ones_rows = jnp.where(
        lax.broadcasted_iota(jnp.int32, (VT_ROWS - V_HEAD_DIM, sub), 0) == 0, 1.0, 0.0)
    for r in range(SUB_TILES):
        rs = slice(r * sub, (r + 1) * sub)
        h = _rms(x_ref[0, rs, :], g_ref[...]).astype(BF16)
        d = _dot(h, wd_ref[...])
        ckv = _rms(d[:, :lora], gckv_ref[...]).astype(BF16)
        kr = _rope_rows(d[:, lora:lora + LANES], d[:, lora + LANES:], gkr_ref[...], gkrs_ref[...],
                        cos_ref[0, rs, :], sin_ref[0, rs, :]).astype(BF16)
        kv = _dot(ckv, wukv_ref[...])
        for hd in range(MLA_HEADS):
            base = hd * (QK_NOPE_DIM + V_HEAD_DIM)
            kn = _rms(kv[:, base:base + QK_NOPE_DIM], gkn_ref[...])
            k_ref[0, hd, rs, :QK_NOPE_DIM] = kn.astype(BF16)
            k_ref[0, hd, rs, QK_NOPE_DIM:] = kr
            v = kv[:, base + QK_NOPE_DIM:base + QK_NOPE_DIM + V_HEAD_DIM]
            vt_ref[0, hd, 0, :, rs] = jnp.concatenate([v.T, ones_rows], axis=0).astype(BF16)


def _swap_rope_cols(w):
    pad = jnp.zeros(w.shape[:-1] + (LANES - QK_ROPE_DIM,), w.dtype)
    x1, x2 = w[..., :ROPE_HALF], w[..., ROPE_HALF:]
    return jnp.concatenate([w, pad], axis=-1), jnp.concatenate([-x2, x1, pad], axis=-1)


def _swap_rope_gain(g):
    pad = jnp.zeros((LANES - QK_ROPE_DIM,), g.dtype)
    straight = jnp.concatenate([g, pad])
    swapped = jnp.concatenate([g[ROPE_HALF:], g[:ROPE_HALF], pad])
    return straight.reshape(1, LANES), swapped.reshape(1, LANES)


def _kv_prep(x, kv_norm_g, w_dkv, g_ckv, w_ukv, w_kr, g_k_nope, g_k_rope, cos, sin, tm, blk):
    b, s, d = x.shape
    lora = w_dkv.shape[1]
    per_blk = blk // tm
    kr_cols, kr_swapped = _swap_rope_cols(w_kr)
    wd = jnp.concatenate([w_dkv, kr_cols, kr_swapped], axis=-1).astype(BF16)
    gkr, gkrs = _swap_rope_gain(g_k_rope)
    tab = pl.BlockSpec((1, tm, LANES), lambda i, j: (i, j, 0))
    return pl.pallas_call(
        _kv_prep_kernel,
        grid=(b, s // tm),
        in_specs=[
            pl.BlockSpec((1, tm, d), lambda i, j: (i, j, 0)),
            _resident((1, d)), _resident((d, lora + 2 * LANES)), _resident((1, lora)),
            _resident(w_ukv.shape), _resident((1, QK_NOPE_DIM)),
            _resident((1, LANES)), _resident((1, LANES)), tab, tab,
        ],
        out_specs=[
            pl.BlockSpec((1, MLA_HEADS, tm, QK_PAD_DIM), lambda i, j: (i, 0, j, 0)),
            pl.BlockSpec((1, MLA_HEADS, 1, VT_ROWS, tm),
                         lambda i, j: (i, 0, j // per_blk, 0, j % per_blk)),
        ],
        out_shape=[
            jax.ShapeDtypeStruct((b, MLA_HEADS, s, QK_PAD_DIM), BF16),
            jax.ShapeDtypeStruct((b, MLA_HEADS, s // blk, VT_ROWS, blk), BF16),
        ],
        compiler_params=_compiler_params(2),
        name="kv_prep",
    )(x, kv_norm_g.reshape(1, d), wd, g_ckv.reshape(1, lora), w_ukv.astype(BF16),
      g_k_nope.reshape(1, QK_NOPE_DIM), gkr, gkrs, cos, sin)


def _q_prep_kernel(x_ref, g_ref, win_ref, gql_ref, wuq_ref, gqn_ref, gqr_ref, gqrs_ref,
                   cos_ref, sin_ref, kbd_ref, vbd_ref, gq_ref, ones_ref, q_ref, ymem_ref,
                   *, score_scale):
    lora = gql_ref.shape[1]
    sub = x_ref.shape[1] // SUB_TILES
    for r in range(SUB_TILES):
        rs = slice(r * sub, (r + 1) * sub)
        h = _rms(x_ref[0, rs, :], g_ref[...]).astype(BF16)
        proj = _dot(h, win_ref[...])
        cq = _rms(proj[:, :lora], gql_ref[...]).astype(BF16)
        qall = _dot(cq, wuq_ref[...])
        cos = cos_ref[0, rs, :]
        sin = sin_ref[0, rs, :]
        for hd in range(MLA_HEADS):
            base = hd * Q_GROUP
            qn = _rms(qall[:, base:base + LANES], gqn_ref[...])
            qr = _rope_rows(qall[:, base + LANES:base + 2 * LANES],
                            qall[:, base + 2 * LANES:base + Q_GROUP],
                            gqr_ref[...], gqrs_ref[...], cos, sin)
            q_ref[0, hd, rs, :QK_NOPE_DIM] = (qn * score_scale).astype(BF16)
            q_ref[0, hd, rs, QK_NOPE_DIM:] = (qr * score_scale).astype(BF16)
        ymem_ref[0, rs, :] = _memory_attention(
            proj[:, lora:], kbd_ref[0, 0], vbd_ref[0, 0], gq_ref[...], ones_ref[...]).astype(BF16)


def _q_prep(x, norm_g, w_in, g_q_lora, w_uq, g_q_nope, g_q_rope, cos, sin, kbd, vbd, gq, ones_bd,
            layer, mla_layer, tm):
    b, s, d = x.shape
    lora = g_q_lora.shape[0]
    w = w_uq.reshape(lora, MLA_HEADS, QK_HEAD_DIM)
    rope_cols, rope_swapped = _swap_rope_cols(w[..., QK_NOPE_DIM:])
    wuq = jnp.concatenate([w[..., :QK_NOPE_DIM], rope_cols, rope_swapped], axis=-1)
    wuq = wuq.reshape(lora, MLA_HEADS * Q_GROUP).astype(BF16)
    gqr, gqrs = _swap_rope_gain(g_q_rope)
    mem_rows = kbd.shape[2]
    mem_spec = pl.BlockSpec((1, 1, mem_rows, MEM_WIDTH), lambda i, j: (layer, i, 0, 0))
    tab = pl.BlockSpec((1, tm, LANES), lambda i, j: (i, j, 0))
    score_scale = QK_HEAD_DIM ** -0.5 * math.log2(math.e)
    return pl.pallas_call(
        functools.partial(_q_prep_kernel, score_scale=score_scale),
        grid=(b, s // tm),
        in_specs=[
            pl.BlockSpec((1, tm, d), lambda i, j: (i, j, 0)),
            _resident(norm_g.shape, (layer, 1)), _resident(w_in.shape, (mla_layer,)),
            _resident((1, lora)),
            _resident((lora, MLA_HEADS * Q_GROUP)), _resident((1, QK_NOPE_DIM)),
            _resident((1, LANES)), _resident((1, LANES)), tab, tab,
            mem_spec, mem_spec, _resident(gq.shape, (layer,)), _resident((MEM_WIDTH, MEM_WIDTH)),
        ],
        out_specs=[
            pl.BlockSpec((1, MLA_HEADS, tm, QK_PAD_DIM), lambda i, j: (i, 0, j, 0)),
            pl.BlockSpec((1, tm, MEM_WIDTH), lambda i, j: (i, j, 0)),
        ],
        out_shape=[
            jax.ShapeDtypeStruct((b, MLA_HEADS, s, QK_PAD_DIM), BF16),
            jax.ShapeDtypeStruct((b, s, MEM_WIDTH), BF16),
        ],
        compiler_params=_compiler_params(2),
        name="q_prep",
    )(x, norm_g, w_in, g_q_lora.reshape(1, lora), wuq, g_q_nope.reshape(1, QK_NOPE_DIM),
      gqr, gqrs, cos, sin, kbd, vbd, gq, ones_bd)


def _flash_items(nq):
    items = [(i, i) for i in range(nq)]
    items += [(i, j) for j in range(nq - 1) for i in range(j + 1, nq)]
    return items


def _emit_steps(first, count, step, unroll=2):
    assert unroll % 2 == 0

    def body(u, carry):
        for k in range(unroll):
            step(first + unroll * u + k, (first + k) % 2)
        return carry

    trips = count // unroll
    if trips:
        lax.fori_loop(0, trips, body, 0)
    for t in range(first + trips * unroll, first + count):
        step(t, t % 2)


def _flash_kernel(qidx_ref, kidx_ref, q_ref, k_ref, vt_ref, o_ref,
                  s0_ref, s1_ref, mc0_ref, mc1_ref, m_ref, acc_ref, *, n_items):
    blk = s0_ref.shape[0]
    nq = m_ref.shape[0]
    s_bufs = (s0_ref, s1_ref)
    mc_bufs = (mc0_ref, mc1_ref)

    def rows(idx):
        return pl.ds(pl.multiple_of(idx * blk, blk), blk)

    half = blk // 2
    key_le_query = (lax.broadcasted_iota(jnp.int32, (half, half), 0)
                    <= lax.broadcasted_iota(jnp.int32, (half, half), 1))

    def half_rows(idx, which):
        return pl.ds(pl.multiple_of(idx * blk + which * half, half), half)

    def scores_full(t, slot):
        s = _dot_nt(k_ref[0, 0, rows(kidx_ref[t]), :], q_ref[0, 0, rows(qidx_ref[t]), :])
        s_bufs[slot][...] = s
        mc_bufs[slot][...] = jnp.max(s, axis=0, keepdims=True)

    def scores_diagonal(t, slot):
        i = qidx_ref[t]
        top = _dot_nt(k_ref[0, 0, half_rows(i, 0), :], q_ref[0, 0, rows(i), :])
        low = _dot_nt(k_ref[0, 0, half_rows(i, 1), :], q_ref[0, 0, half_rows(i, 1), :])
        top_left = jnp.where(key_le_query, top[:, :half], MASK_VALUE)
        low_right = jnp.where(key_le_query, low, MASK_VALUE)
        s_bufs[slot][:half, :half] = top_left
        s_bufs[slot][:half, half:] = top[:, half:]
        s_bufs[slot][half:, half:] = low_right
        mc_bufs[slot][:, :half] = jnp.max(top_left, axis=0, keepdims=True)
        mc_bufs[slot][:, half:] = jnp.maximum(jnp.max(top[:, half:], axis=0, keepdims=True),
                                              jnp.max(low_right, axis=0, keepdims=True))

    def absorb_full(t, slot):
        qi = qidx_ref[t]
        m_prev = m_ref[qi]
        m_new = jnp.maximum(m_prev, mc_bufs[slot][...])
        alpha = jnp.exp2(m_prev - m_new)
        p = jnp.exp2(s_bufs[slot][...] - m_new).astype(BF16)
        acc_ref[qi] = alpha * acc_ref[qi] + _dot(vt_ref[0, 0, kidx_ref[t]], p)
        m_ref[qi] = m_new

    def absorb_diagonal(t, slot):
        qi = qidx_ref[t]
        m_new = mc_bufs[slot][...]
        p_top = jnp.exp2(s_bufs[slot][:half, :] - m_new).astype(BF16)
        p_low = jnp.exp2(s_bufs[slot][half:, half:] - m_new[:, half:]).astype(BF16)
        vt = vt_ref[0, 0, qi]
        pv_top = _dot(vt[:, :half], p_top)
        pv_low = _dot(vt[:, half:], p_low)
        acc_ref[qi, :, :half] = pv_top[:, :half]
        acc_ref[qi, :, half:] = pv_top[:, half:] + pv_low
        m_ref[qi] = m_new

    def scores(t, slot, diagonal):
        (scores_diagonal if diagonal else scores_full)(t, slot)

    def absorb(t, slot, diagonal):
        (absorb_diagonal if diagonal else absorb_full)(t, slot)

    def step(absorb_is_diagonal, scores_is_diagonal):
        def run(t, slot):
            scores(t + 1, 1 - slot, scores_is_diagonal)
            absorb(t, slot, absorb_is_diagonal)
        return run

    scores(0, 0, True)
    _emit_steps(0, nq - 1, step(True, True))
    if n_items > nq:
        _emit_steps(nq - 1, 1, step(True, False))
        _emit_steps(nq, n_items - 1 - nq, step(False, False), unroll=FLASH_UNROLL)
    absorb(n_items - 1, (n_items - 1) % 2, n_items == nq)

    def finish(qi, carry):
        acc = acc_ref[qi]
        out_t = acc[:V_HEAD_DIM, :] / acc[V_HEAD_DIM:V_HEAD_DIM + 1, :]
        o_ref[0, rows(qi), :] = out_t.T.astype(BF16)
        return carry

    lax.fori_loop(0, nq, finish, 0)


def _flash(q, k, vt, blk):
    b, heads, s, _ = q.shape
    nq = s // blk
    items = _flash_items(nq)
    qidx = jnp.asarray([i for i, _ in items], jnp.int32)
    kidx = jnp.asarray([j for _, j in items], jnp.int32)
    grid_spec = pltpu.PrefetchScalarGridSpec(
        num_scalar_prefetch=2,
        grid=(b, heads),
        in_specs=[
            pl.BlockSpec((1, 1, s, QK_PAD_DIM), lambda i, h, *_: (i, h, 0, 0)),
            pl.BlockSpec((1, 1, s, QK_PAD_DIM), lambda i, h, *_: (i, h, 0, 0)),
            pl.BlockSpec((1, 1, nq, VT_ROWS, blk), lambda i, h, *_: (i, h, 0, 0, 0)),
        ],
        out_specs=pl.BlockSpec((1, s, V_HEAD_DIM), lambda i, h, *_: (i, 0, h)),
        scratch_shapes=[
            pltpu.VMEM((blk, blk), F32), pltpu.VMEM((blk, blk), F32),
            pltpu.VMEM((1, blk), F32), pltpu.VMEM((1, blk), F32),
            pltpu.VMEM((nq, 1, blk), F32), pltpu.VMEM((nq, VT_ROWS, blk), F32),
        ],
    )
    return pl.pallas_call(
        functools.partial(_flash_kernel, n_items=len(items)),
        grid_spec=grid_spec,
        out_shape=jax.ShapeDtypeStruct((b, s, heads * V_HEAD_DIM), BF16),
        compiler_params=_compiler_params(2),
        name="flash",
    )(qidx, kidx, q, k, vt)


def kernel(x, mem, positions, norm_g, ffn_w13, ffn_w2, w_out, mem_norm_g, w_mem_kv, g_mem_q, g_mem_k,
           conv_w_in, conv_w, mla_w_in, g_q_lora, w_uq, g_q_nope, g_q_rope,
           kv_norm_g, w_dkv, g_ckv, w_ukv, w_kr, g_k_nope, g_k_rope):
    b, s, d = x.shape
    depth = norm_g.shape[0]
    n_conv = conv_w_in.shape[0]
    t = b * s
    tm = min(TOKEN_TILE, s)
    blk = min(FLASH_BLOCK, s)
    assert s % blk == 0 and blk % tm == 0 and d - conv_w.shape[2] == MEM_WIDTH

    head_of = jnp.arange(MEM_WIDTH, dtype=jnp.int32) // MEM_HEAD_DIM
    ones_bd = (head_of[:, None] == head_of[None, :]).astype(BF16)
    kbd, vbd = _mem_kv(mem, mem_norm_g, w_mem_kv, g_mem_k, ones_bd)
    gq_all = jnp.tile(g_mem_q, (1, MEM_HEADS)).reshape(depth, 1, MEM_WIDTH)
    cos, sin = _rope_tables(positions, tm)
    cos = cos.reshape(b, s, LANES)
    sin = sin.reshape(b, s, LANES)

    w13 = ffn_w13.astype(BF16)
    w2 = ffn_w2.astype(BF16)
    wo = w_out.astype(BF16)
    conv_in = conv_w_in.astype(BF16)
    mla_in = mla_w_in.astype(BF16)
    gn = norm_g.reshape(depth, norm_g.shape[1], 1, d)

    xf = x.reshape(t, d)
    shared = None
    for layer in range(depth):
        if layer == n_conv:
            shared = _kv_prep(xf.reshape(b, s, d), kv_norm_g, w_dkv, g_ckv, w_ukv, w_kr, g_k_nope,
                              g_k_rope, cos, sin, min(MIXER_TILE, blk), blk)
        xf = _ffn(xf, gn, w13, w2, layer, min(FFN_TILE, s))
        x3 = xf.reshape(b, s, d)
        if layer < n_conv:
            ymix, ymem = _conv_mixer(x3, gn, conv_in, conv_w, kbd, vbd, gq_all, ones_bd, layer,
                                     min(MIXER_TILE, s))
        else:
            j = layer - n_conv
            q, ymem = _q_prep(x3, gn, mla_in, g_q_lora[j], w_uq[j], g_q_nope[j], g_q_rope[j],
                              cos, sin, kbd, vbd, gq_all, ones_bd, layer, j, min(MIXER_TILE, s))
            ymix = _flash(q, shared[0], shared[1], blk)
        xf = _proj_ffn(xf, ymix.reshape(t, -1), ymem.reshape(t, MEM_WIDTH), wo, gn, w13, w2,
                       layer, min(FFN_TILE, s))
    return xf.reshape(b, s, d)
```

```python
import functools
import math

import jax
import jax.numpy as jnp
from jax import lax
from jax.experimental import pallas as pl
from jax.experimental.pallas import tpu as pltpu

F32 = jnp.float32
BF16 = jnp.bfloat16

EPS = 1e-6
FFN_HALF = 0.5
MEM_HEADS = 4
MEM_HEAD_DIM = 64
MEM_WIDTH = MEM_HEADS * MEM_HEAD_DIM
CONV_WIDTH = 3
MLA_HEADS = 6
QK_NOPE_DIM = 128
QK_ROPE_DIM = 64
ROPE_HALF = QK_ROPE_DIM // 2
QK_HEAD_DIM = QK_NOPE_DIM + QK_ROPE_DIM
V_HEAD_DIM = 128
ROPE_THETA = 10000.0

LANES = 128
SUBLANES = 8
ROPE_GROUPS = LANES // ROPE_HALF
QK_PAD_DIM = 2 * LANES
Q_GROUP = 3 * LANES
BF16_SUBLANES = 2 * SUBLANES
VT_ROWS = V_HEAD_DIM + BF16_SUBLANES
V7X_VMEM_BYTES = 64 * 1024 * 1024
VMEM_LIMIT_BYTES = V7X_VMEM_BYTES - 8 * 1024 * 1024
MASK_VALUE = -1e30

TOKEN_TILE = 512
MIXER_TILE = 1024
SUB_TILES = 2
FFN_TILE = 1024
FFN_SUB_ROWS = 256
FLASH_BLOCK = 1024
FLASH_UNROLL = 4


def _compiler_params(n_axes):
    return pltpu.CompilerParams(
        dimension_semantics=("arbitrary",) * n_axes, vmem_limit_bytes=VMEM_LIMIT_BYTES)


def _resident(shape, lead=()):
    block = (None,) * len(lead) + tuple(shape[len(lead):])
    index = tuple(lead) + (0,) * (len(shape) - len(lead))
    return pl.BlockSpec(block, lambda *_: index, pipeline_mode=pl.Buffered(1))


def _dot(a, b):
    return jnp.dot(a, b, preferred_element_type=F32)


def _dot_nt(a, b):
    return lax.dot_general(a, b, (((1,), (1,)), ((), ())), preferred_element_type=F32)


def _rms(x, g):
    return x * lax.rsqrt(jnp.mean(x * x, axis=-1, keepdims=True) + EPS) * g


def _group_sum(sq, ones_bd):
    hi = sq.astype(BF16)
    lo = (sq - hi.astype(F32)).astype(BF16)
    return _dot(hi, ones_bd) + _dot(lo, ones_bd)


def _rope_tables_kernel(pos_ref, inv_freq_ref, cos_ref, sin_ref):
    tq = pos_ref.shape[0]
    group = lax.broadcasted_iota(jnp.int32, (tq, LANES), 1) // ROPE_HALF
    pos = pos_ref[...].astype(F32)
    pos_lanes = pos[:, ROPE_GROUPS - 1:ROPE_GROUPS]
    for i in range(ROPE_GROUPS - 2, -1, -1):
        pos_lanes = jnp.where(group == i, pos[:, i:i + 1], pos_lanes)
    ang = pos_lanes * inv_freq_ref[...]
    for table, out_ref in ((jnp.cos(ang), cos_ref), (jnp.sin(ang), sin_ref)):
        rolled = [table] + [pltpu.roll(table, ROPE_HALF * k, 1) for k in range(1, ROPE_GROUPS)]
        for i in range(ROPE_GROUPS):
            spread = rolled[(ROPE_GROUPS - 1 - i) % ROPE_GROUPS]
            for j in range(ROPE_GROUPS - 2, -1, -1):
                spread = jnp.where(group == j, rolled[(j - i) % ROPE_GROUPS], spread)
            out_ref[i] = spread


def _rope_tables(positions):
    t = positions.size
    quarter = t // ROPE_GROUPS
    tq = quarter if quarter <= 2 * TOKEN_TILE else TOKEN_TILE
    assert t % ROPE_GROUPS == 0 and quarter % tq == 0
    pos = positions.reshape(ROPE_GROUPS, quarter).T
    inv_freq = ROPE_THETA ** (-jnp.arange(0, QK_ROPE_DIM, 2, dtype=F32) / QK_ROPE_DIM)
    inv_freq = jnp.tile(inv_freq, ROPE_GROUPS).reshape(1, LANES)
    out = jax.ShapeDtypeStruct((ROPE_GROUPS, quarter, LANES), F32)
    out_spec = pl.BlockSpec((ROPE_GROUPS, tq, LANES), lambda i: (0, i, 0))
    cos, sin = pl.pallas_call(
        _rope_tables_kernel,
        grid=(quarter // tq,),
        in_specs=[pl.BlockSpec((tq, ROPE_GROUPS), lambda i: (i, 0)), _resident((1, LANES))],
        out_specs=[out_spec, out_spec],
        out_shape=[out, out],
        compiler_params=_compiler_params(1),
        name="rope_tables",
    )(pos, inv_freq)
    return cos.reshape(t, LANES), sin.reshape(t, LANES)


def _mem_kv_kernel(mem_ref, g_ref, w_ref, gk_ref, ones_ref, kbd_ref, vbd_ref):
    m = mem_ref.shape[1]
    hm = _rms(mem_ref[0], g_ref[0]).astype(BF16)
    kv = _dot(hm, w_ref[0])
    k = kv[:, :MEM_WIDTH]
    v = kv[:, MEM_WIDTH:]
    ms = _group_sum(k * k, ones_ref[...]) * (1.0 / MEM_HEAD_DIM)
    kn = k * lax.rsqrt(ms + EPS) * gk_ref[0] * (MEM_HEAD_DIM ** -0.5)
    head_of_lane = lax.broadcasted_iota(jnp.int32, (m, MEM_WIDTH), 1) // MEM_HEAD_DIM
    for h in range(MEM_HEADS):
        sel = head_of_lane == h
        kbd_ref[0, 0, h * m:(h + 1) * m, :] = jnp.where(sel, kn, 0.0).astype(BF16)
        vbd_ref[0, 0, h * m:(h + 1) * m, :] = jnp.where(sel, v, 0.0).astype(BF16)


def _mem_kv(mem, mem_norm_g, w_mem_kv, g_mem_k, ones_bd):
    b, m, d = mem.shape
    depth = w_mem_kv.shape[0]
    gk = jnp.tile(g_mem_k, (1, MEM_HEADS)).reshape(depth, 1, MEM_WIDTH)
    out = jax.ShapeDtypeStruct((depth, b, MEM_HEADS * m, MEM_WIDTH), BF16)
    out_spec = pl.BlockSpec((1, 1, MEM_HEADS * m, MEM_WIDTH), lambda l, i: (l, i, 0, 0))
    return pl.pallas_call(
        _mem_kv_kernel,
        grid=(depth, b),
        in_specs=[
            pl.BlockSpec((1, m, d), lambda l, i: (i, 0, 0)),
            pl.BlockSpec((1, 1, d), lambda l, i: (l, 0, 0)),
            pl.BlockSpec((1, d, 2 * MEM_WIDTH), lambda l, i: (l, 0, 0)),
            pl.BlockSpec((1, 1, MEM_WIDTH), lambda l, i: (l, 0, 0)),
            _resident((MEM_WIDTH, MEM_WIDTH)),
        ],
        out_specs=[out_spec, out_spec],
        out_shape=[out, out],
        compiler_params=_compiler_params(2),
        name="mem_kv",
    )(mem, mem_norm_g.reshape(depth, 1, d), w_mem_kv.astype(BF16), gk, ones_bd)


def _memory_attention(q, kbd, vbd, gq, ones_bd):
    tm = q.shape[0]
    m = kbd.shape[0] // MEM_HEADS
    ms = _group_sum(q * q, ones_bd) * (1.0 / MEM_HEAD_DIM)
    qn = (q * lax.rsqrt(ms + EPS) * gq).astype(BF16)
    s = _dot_nt(qn, kbd)
    probs = []
    denoms = []
    for h in range(MEM_HEADS):
        sh = s[:, h * m:(h + 1) * m]
        p = jnp.exp(sh - jnp.max(sh, axis=-1, keepdims=True))
        denoms.append(jnp.sum(p, axis=-1, keepdims=True))
        probs.append(p.astype(BF16))
    o = _dot(jnp.concatenate(probs, axis=-1), vbd)
    head_of_lane = lax.broadcasted_iota(jnp.int32, (tm, MEM_WIDTH), 1) // MEM_HEAD_DIM
    denom = denoms[MEM_HEADS - 1]
    for h in range(MEM_HEADS - 2, -1, -1):
        denom = jnp.where(head_of_lane == h, denoms[h], denom)
    return o / denom


def _swiglu_step(x, g_ref, w13_ref, w2_ref):
    d_ff = w2_ref.shape[0]
    h = _rms(x, g_ref[...]).astype(BF16)
    gu = _dot(h, w13_ref[...])
    gate = gu[:, :d_ff]
    up = gu[:, d_ff:]
    act = (gate * jax.nn.sigmoid(gate) * up).astype(BF16)
    return x + FFN_HALF * _dot(act, w2_ref[...])


def _ffn_kernel(x_ref, g_ref, w13_ref, w2_ref, o_ref):
    for r in range(x_ref.shape[0] // FFN_SUB_ROWS):
        rs = slice(r * FFN_SUB_ROWS, (r + 1) * FFN_SUB_ROWS)
        o_ref[rs, :] = _swiglu_step(x_ref[rs, :], g_ref, w13_ref, w2_ref)


def _proj_ffn_kernel(x_ref, ymix_ref, ymem_ref, wout_ref, g_ref, w13_ref, w2_ref, o_ref):
    mix = ymix_ref.shape[1]
    for r in range(x_ref.shape[0] // FFN_SUB_ROWS):
        rs = slice(r * FFN_SUB_ROWS, (r + 1) * FFN_SUB_ROWS)
        x1 = (x_ref[rs, :] + _dot(ymix_ref[rs, :], wout_ref[:mix, :])
              + _dot(ymem_ref[rs, :], wout_ref[mix:, :]))
        o_ref[rs, :] = _swiglu_step(x1, g_ref, w13_ref, w2_ref)


def _ffn(x, norm_g, w13, w2, layer, tm):
    t, d = x.shape
    row = pl.BlockSpec((tm, d), lambda i: (i, 0))
    return pl.pallas_call(
        _ffn_kernel,
        grid=(t // tm,),
        in_specs=[row, _resident(norm_g.shape, (layer, 0)), _resident(w13.shape, (layer, 0)),
                  _resident(w2.shape, (layer, 0))],
        out_specs=row,
        out_shape=jax.ShapeDtypeStruct((t, d), F32),
        compiler_params=_compiler_params(1),
        name="ffn",
    )(x, norm_g, w13, w2)


def _proj_ffn(x, ymix, ymem, w_out, norm_g, w13, w2, layer, tm):
    t, d = x.shape
    mix = ymix.shape[1]
    row = pl.BlockSpec((tm, d), lambda i: (i, 0))
    return pl.pallas_call(
        _proj_ffn_kernel,
        grid=(t // tm,),
        in_specs=[
            row,
            pl.BlockSpec((tm, mix), lambda i: (i, 0)),
            pl.BlockSpec((tm, d - mix), lambda i: (i, 0)),
            _resident(w_out.shape, (layer,)), _resident(norm_g.shape, (layer, 2)),
            _resident(w13.shape, (layer, 1)), _resident(w2.shape, (layer, 1)),
        ],
        out_specs=row,
        out_shape=jax.ShapeDtypeStruct((t, d), F32),
        compiler_params=_compiler_params(1),
        name="proj_ffn",
    )(x, ymix, ymem, w_out, norm_g, w13, w2)


def _conv_mixer_kernel(x_ref, g_ref, win_ref, convw_ref, kbd_ref, vbd_ref, gq_ref, ones_ref,
                       ymix_ref, ymem_ref, carry_ref):
    tm = x_ref.shape[1]
    ch = ymix_ref.shape[2]

    @pl.when(pl.program_id(1) == 0)
    def _():
        carry_ref[...] = jnp.zeros_like(carry_ref)

    sub = tm // SUB_TILES
    w = convw_ref[...]
    row = lax.broadcasted_iota(jnp.int32, (sub, ch), 0)
    tail = carry_ref[...]
    for r in range(SUB_TILES):
        rs = slice(r * sub, (r + 1) * sub)
        h = _rms(x_ref[0, rs, :], g_ref[...]).astype(BF16)
        q_mem = _dot(h, win_ref[:, 3 * ch:])
        ymem_ref[0, rs, :] = _memory_attention(
            q_mem, kbd_ref[0, 0], vbd_ref[0, 0], gq_ref[...], ones_ref[...]).astype(BF16)
        proj = _dot(h, win_ref[:, :3 * ch])
        gate_b = proj[:, :ch]
        u = proj[:, ch:2 * ch] * proj[:, 2 * ch:3 * ch]
        prev2 = tail[SUBLANES - 2:SUBLANES - 1, :]
        prev1 = tail[SUBLANES - 1:SUBLANES, :]
        u1 = jnp.where(row == 0, prev1, pltpu.roll(u, 1, 0))
        u2 = jnp.where(row == 0, prev2, jnp.where(row == 1, prev1, pltpu.roll(u, 2, 0)))
        tail = u[sub - SUBLANES:, :]
        conv = w[0:1, :] * u2 + w[1:2, :] * u1 + w[2:3, :] * u
        ymix_ref[0, rs, :] = (gate_b * conv).astype(BF16)
    carry_ref[...] = tail


def _conv_mixer(x, norm_g, w_in, conv_w, kbd, vbd, gq, ones_bd, layer, tm):
    b, s, d = x.shape
    ch = conv_w.shape[2]
    mem_rows = kbd.shape[2]
    mem_spec = pl.BlockSpec((1, 1, mem_rows, MEM_WIDTH), lambda i, j: (layer, i, 0, 0))
    return pl.pallas_call(
        _conv_mixer_kernel,
        grid=(b, s // tm),
        in_specs=[
            pl.BlockSpec((1, tm, d), lambda i, j: (i, j, 0)),
            _resident(norm_g.shape, (layer, 1)), _resident(w_in.shape, (layer,)),
            _resident(conv_w.shape, (layer,)),
            mem_spec, mem_spec, _resident(gq.shape, (layer,)), _resident((MEM_WIDTH, MEM_WIDTH)),
        ],
        out_specs=[
            pl.BlockSpec((1, tm, ch), lambda i, j: (i, j, 0)),
            pl.BlockSpec((1, tm, MEM_WIDTH), lambda i, j: (i, j, 0)),
        ],
        out_shape=[
            jax.ShapeDtypeStruct((b, s, ch), BF16),
            jax.ShapeDtypeStruct((b, s, MEM_WIDTH), BF16),
        ],
        scratch_shapes=[pltpu.VMEM((SUBLANES, ch), F32)],
        compiler_params=_compiler_params(2),
        name="conv_mixer",
    )(x, norm_g, w_in, conv_w, kbd, vbd, gq, ones_bd)


def _rope_rows(r, r_swapped, g, g_swapped, cos, sin):
    inv = lax.rsqrt(jnp.sum(r * r, axis=-1, keepdims=True) * (1.0 / QK_ROPE_DIM) + EPS)
    return (r * g * cos + r_swapped * g_swapped * sin) * inv


def _kv_prep_kernel(x_ref, g_ref, wd_ref, gckv_ref, wukv_ref, gkn_ref, gkr_ref, gkrs_ref,
                    cos_ref, sin_ref, k_ref, vt_ref):
    sub = x_ref.shape[1] // SUB_TILES
    lora = gckv_ref.shape[1]
    ones_rows = jnp.where(
        lax.broadcasted_iota(jnp.int32, (VT_ROWS - V_HEAD_DIM, sub), 0) == 0, 1.0, 0.0)
    for r in range(SUB_TILES):
        rs = slice(r * sub, (r + 1) * sub)
        h = _rms(x_ref[0, rs, :], g_ref[...]).astype(BF16)
        d = _dot(h, wd_ref[...])
        ckv = _rms(d[:, :lora], gckv_ref[...]).astype(BF16)
        kr = _rope_rows(d[:, lora:lora + LANES], d[:, lora + LANES:], gkr_ref[...], gkrs_ref[...],
                        cos_ref[0, rs, :], sin_ref[0, rs, :]).astype(BF16)
        kv = _dot(ckv, wukv_ref[...])
        for hd in range(MLA_HEADS):
            base = hd * (QK_NOPE_DIM + V_HEAD_DIM)
            kn = _rms(kv[:, base:base + QK_NOPE_DIM], gkn_ref[...])
            k_ref[0, hd, rs, :QK_NOPE_DIM] = kn.astype(BF16)
            k_ref[0, hd, rs, QK_NOPE_DIM:] = kr
            v = kv[:, base + QK_NOPE_DIM:base + QK_NOPE_DIM + V_HEAD_DIM]
            vt_ref[0, hd, 0, :, rs] = jnp.concatenate([v.T, ones_rows], axis=0).astype(BF16)


def _swap_rope_cols(w):
    pad = jnp.zeros(w.shape[:-1] + (LANES - QK_ROPE_DIM,), w.dtype)
    x1, x2 = w[..., :ROPE_HALF], w[..., ROPE_HALF:]
    return jnp.concatenate([w, pad], axis=-1), jnp.concatenate([-x2, x1, pad], axis=-1)


def _swap_rope_gain(g):
    pad = jnp.zeros((LANES - QK_ROPE_DIM,), g.dtype)
    straight = jnp.concatenate([g, pad])
    swapped = jnp.concatenate([g[ROPE_HALF:], g[:ROPE_HALF], pad])
    return straight.reshape(1, LANES), swapped.reshape(1, LANES)


def _kv_prep(x, kv_norm_g, w_dkv, g_ckv, w_ukv, w_kr, g_k_nope, g_k_rope, cos, sin, tm, blk):
    b, s, d = x.shape
    lora = w_dkv.shape[1]
    per_blk = blk // tm
    kr_cols, kr_swapped = _swap_rope_cols(w_kr)
    wd = jnp.concatenate([w_dkv, kr_cols, kr_swapped], axis=-1).astype(BF16)
    gkr, gkrs = _swap_rope_gain(g_k_rope)
    tab = pl.BlockSpec((1, tm, LANES), lambda i, j: (i, j, 0))
    return pl.pallas_call(
        _kv_prep_kernel,
        grid=(b, s // tm),
        in_specs=[
            pl.BlockSpec((1, tm, d), lambda i, j: (i, j, 0)),
            _resident((1, d)), _resident((d, lora + 2 * LANES)), _resident((1, lora)),
            _resident(w_ukv.shape), _resident((1, QK_NOPE_DIM)),
            _resident((1, LANES)), _resident((1, LANES)), tab, tab,
        ],
        out_specs=[
            pl.BlockSpec((1, MLA_HEADS, tm, QK_PAD_DIM), lambda i, j: (i, 0, j, 0)),
            pl.BlockSpec((1, MLA_HEADS, 1, VT_ROWS, tm),
                         lambda i, j: (i, 0, j // per_blk, 0, j % per_blk)),
        ],
        out_shape=[
            jax.ShapeDtypeStruct((b, MLA_HEADS, s, QK_PAD_DIM), BF16),
            jax.ShapeDtypeStruct((b, MLA_HEADS, s // blk, VT_ROWS, blk), BF16),
        ],
        compiler_params=_compiler_params(2),
        name="kv_prep",
    )(x, kv_norm_g.reshape(1, d), wd, g_ckv.reshape(1, lora), w_ukv.astype(BF16),
      g_k_nope.reshape(1, QK_NOPE_DIM), gkr, gkrs, cos, sin)


def _q_prep_kernel(x_ref, g_ref, win_ref, gql_ref, wuq_ref, gqn_ref, gqr_ref, gqrs_ref,
                   cos_ref, sin_ref, kbd_ref, vbd_ref, gq_ref, ones_ref, q_ref, ymem_ref,
                   *, score_scale):
    lora = gql_ref.shape[1]
    sub = x_ref.shape[1] // SUB_TILES
    for r in range(SUB_TILES):
        rs = slice(r * sub, (r + 1) * sub)
        h = _rms(x_ref[0, rs, :], g_ref[...]).astype(BF16)
        proj = _dot(h, win_ref[...])
        cq = _rms(proj[:, :lora], gql_ref[...]).astype(BF16)
        qall = _dot(cq, wuq_ref[...])
        cos = cos_ref[0, rs, :]
        sin = sin_ref[0, rs, :]
        for hd in range(MLA_HEADS):
            base = hd * Q_GROUP
            qn = _rms(qall[:, base:base + LANES], gqn_ref[...])
            qr = _rope_rows(qall[:, base + LANES:base + 2 * LANES],
                            qall[:, base + 2 * LANES:base + Q_GROUP],
                            gqr_ref[...], gqrs_ref[...], cos, sin)
            q_ref[0, hd, rs, :QK_NOPE_DIM] = (qn * score_scale).astype(BF16)
            q_ref[0, hd, rs, QK_NOPE_DIM:] = (qr * score_scale).astype(BF16)
        ymem_ref[0, rs, :] = _memory_attention(
            proj[:, lora:], kbd_ref[0, 0], vbd_ref[0, 0], gq_ref[...], ones_ref[...]).astype(BF16)


def _q_prep(x, norm_g, w_in, g_q_lora, w_uq, g_q_nope, g_q_rope, cos, sin, kbd, vbd, gq, ones_bd,
            layer, mla_layer, tm):
    b, s, d = x.shape
    lora = g_q_lora.shape[0]
    w = w_uq.reshape(lora, MLA_HEADS, QK_HEAD_DIM)
    rope_cols, rope_swapped = _swap_rope_cols(w[..., QK_NOPE_DIM:])
    wuq = jnp.concatenate([w[..., :QK_NOPE_DIM], rope_cols, rope_swapped], axis=-1)
    wuq = wuq.reshape(lora, MLA_HEADS * Q_GROUP).astype(BF16)
    gqr, gqrs = _swap_rope_gain(g_q_rope)
    mem_rows = kbd.shape[2]
    mem_spec = pl.BlockSpec((1, 1, mem_rows, MEM_WIDTH), lambda i, j: (layer, i, 0, 0))
    tab = pl.BlockSpec((1, tm, LANES), lambda i, j: (i, j, 0))
    score_scale = QK_HEAD_DIM ** -0.5 * math.log2(math.e)
    return pl.pallas_call(
        functools.partial(_q_prep_kernel, score_scale=score_scale),
        grid=(b, s // tm),
        in_specs=[
            pl.BlockSpec((1, tm, d), lambda i, j: (i, j, 0)),
            _resident(norm_g.shape, (layer, 1)), _resident(w_in.shape, (mla_layer,)),
            _resident((1, lora)),
            _resident((lora, MLA_HEADS * Q_GROUP)), _resident((1, QK_NOPE_DIM)),
            _resident((1, LANES)), _resident((1, LANES)), tab, tab,
            mem_spec, mem_spec, _resident(gq.shape, (layer,)), _resident((MEM_WIDTH, MEM_WIDTH)),
        ],
        out_specs=[
            pl.BlockSpec((1, MLA_HEADS, tm, QK_PAD_DIM), lambda i, j: (i, 0, j, 0)),
            pl.BlockSpec((1, tm, MEM_WIDTH), lambda i, j: (i, j, 0)),
        ],
        out_shape=[
            jax.ShapeDtypeStruct((b, MLA_HEADS, s, QK_PAD_DIM), BF16),
            jax.ShapeDtypeStruct((b, s, MEM_WIDTH), BF16),
        ],
        compiler_params=_compiler_params(2),
        name="q_prep",
    )(x, norm_g, w_in, g_q_lora.reshape(1, lora), wuq, g_q_nope.reshape(1, QK_NOPE_DIM),
      gqr, gqrs, cos, sin, kbd, vbd, gq, ones_bd)


def _flash_items(nq):
    items = [(i, i) for i in range(nq)]
    items += [(i, j) for j in range(nq - 1) for i in range(j + 1, nq)]
    return items


def _emit_steps(first, count, step, unroll=2):
    assert unroll % 2 == 0

    def body(u, carry):
        for k in range(unroll):
            step(first + unroll * u + k, (first + k) % 2)
        return carry

    trips = count // unroll
    if trips:
        lax.fori_loop(0, trips, body, 0)
    for t in range(first + trips * unroll, first + count):
        step(t, t % 2)


def _flash_kernel(qidx_ref, kidx_ref, q_ref, k_ref, vt_ref, o_ref,
                  s0_ref, s1_ref, mc0_ref, mc1_ref, m_ref, acc_ref, *, n_items):
    blk = s0_ref.shape[0]
    nq = m_ref.shape[0]
    s_bufs = (s0_ref, s1_ref)
    mc_bufs = (mc0_ref, mc1_ref)

    def rows(idx):
        return pl.ds(pl.multiple_of(idx * blk, blk), blk)

    half = blk // 2
    key_le_query = (lax.broadcasted_iota(jnp.int32, (half, half), 0)
                    <= lax.broadcasted_iota(jnp.int32, (half, half), 1))

    def half_rows(idx, which):
        return pl.ds(pl.multiple_of(idx * blk + which * half, half), half)

    def scores_full(t, slot):
        s = _dot_nt(k_ref[0, 0, rows(kidx_ref[t]), :], q_ref[0, 0, rows(qidx_ref[t]), :])
        s_bufs[slot][...] = s
        mc_bufs[slot][...] = jnp.max(s, axis=0, keepdims=True)

    def scores_diagonal(t, slot):
        i = qidx_ref[t]
        top = _dot_nt(k_ref[0, 0, half_rows(i, 0), :], q_ref[0, 0, rows(i), :])
        low = _dot_nt(k_ref[0, 0, half_rows(i, 1), :], q_ref[0, 0, half_rows(i, 1), :])
        top_left = jnp.where(key_le_query, top[:, :half], MASK_VALUE)
        low_right = jnp.where(key_le_query, low, MASK_VALUE)
        s_bufs[slot][:half, :half] = top_left
        s_bufs[slot][:half, half:] = top[:, half:]
        s_bufs[slot][half:, half:] = low_right
        mc_bufs[slot][:, :half] = jnp.max(top_left, axis=0, keepdims=True)
        mc_bufs[slot][:, half:] = jnp.maximum(jnp.max(top[:, half:], axis=0, keepdims=True),
                                              jnp.max(low_right, axis=0, keepdims=True))

    def absorb_full(t, slot):
        qi = qidx_ref[t]
        m_prev = m_ref[qi]
        m_new = jnp.maximum(m_prev, mc_bufs[slot][...])
        alpha = jnp.exp2(m_prev - m_new)
        p = jnp.exp2(s_bufs[slot][...] - m_new).astype(BF16)
        acc_ref[qi] = alpha * acc_ref[qi] + _dot(vt_ref[0, 0, kidx_ref[t]], p)
        m_ref[qi] = m_new

    def absorb_diagonal(t, slot):
        qi = qidx_ref[t]
        m_new = mc_bufs[slot][...]
        p_top = jnp.exp2(s_bufs[slot][:half, :] - m_new).astype(BF16)
        p_low = jnp.exp2(s_bufs[slot][half:, half:] - m_new[:, half:]).astype(BF16)
        vt = vt_ref[0, 0, qi]
        pv_top = _dot(vt[:, :half], p_top)
        pv_low = _dot(vt[:, half:], p_low)
        acc_ref[qi, :, :half] = pv_top[:, :half]
        acc_ref[qi, :, half:] = pv_top[:, half:] + pv_low
        m_ref[qi] = m_new

    def scores(t, slot, diagonal):
        (scores_diagonal if diagonal else scores_full)(t, slot)

    def absorb(t, slot, diagonal):
        (absorb_diagonal if diagonal else absorb_full)(t, slot)

    def step(absorb_is_diagonal, scores_is_diagonal):
        def run(t, slot):
            scores(t + 1, 1 - slot, scores_is_diagonal)
            absorb(t, slot, absorb_is_diagonal)
        return run

    scores(0, 0, True)
    _emit_steps(0, nq - 1, step(True, True))
    if n_items > nq:
        _emit_steps(nq - 1, 1, step(True, False))
        _emit_steps(nq, n_items - 1 - nq, step(False, False), unroll=FLASH_UNROLL)
    absorb(n_items - 1, (n_items - 1) % 2, n_items == nq)

    def finish(qi, carry):
        acc = acc_ref[qi]
        out_t = acc[:V_HEAD_DIM, :] / acc[V_HEAD_DIM:V_HEAD_DIM + 1, :]
        o_ref[0, rows(qi), :] = out_t.T.astype(BF16)
        return carry

    lax.fori_loop(0, nq, finish, 0)


def _flash(q, k, vt, blk):
    b, heads, s, _ = q.shape
    nq = s // blk
    items = _flash_items(nq)
    qidx = jnp.asarray([i for i, _ in items], jnp.int32)
    kidx = jnp.asarray([j for _, j in items], jnp.int32)
    grid_spec = pltpu.PrefetchScalarGridSpec(
        num_scalar_prefetch=2,
        grid=(b, heads),
        in_specs=[
            pl.BlockSpec((1, 1, s, QK_PAD_DIM), lambda i, h, *_: (i, h, 0, 0)),
            pl.BlockSpec((1, 1, s, QK_PAD_DIM), lambda i, h, *_: (i, h, 0, 0)),
            pl.BlockSpec((1, 1, nq, VT_ROWS, blk), lambda i, h, *_: (i, h, 0, 0, 0)),
        ],
        out_specs=pl.BlockSpec((1, s, V_HEAD_DIM), lambda i, h, *_: (i, 0, h)),
        scratch_shapes=[
            pltpu.VMEM((blk, blk), F32), pltpu.VMEM((blk, blk), F32),
            pltpu.VMEM((1, blk), F32), pltpu.VMEM((1, blk), F32),
            pltpu.VMEM((nq, 1, blk), F32), pltpu.VMEM((nq, VT_ROWS, blk), F32),
        ],
    )
    return pl.pallas_call(
        functools.partial(_flash_kernel, n_items=len(items)),
        grid_spec=grid_spec,
        out_shape=jax.ShapeDtypeStruct((b, s, heads * V_HEAD_DIM), BF16),
        compiler_params=_compiler_params(2),
        name="flash",
    )(qidx, kidx, q, k, vt)


def kernel(x, mem, positions, norm_g, ffn_w13, ffn_w2, w_out, mem_norm_g, w_mem_kv, g_mem_q, g_mem_k,
           conv_w_in, conv_w, mla_w_in, g_q_lora, w_uq, g_q_nope, g_q_rope,
           kv_norm_g, w_dkv, g_ckv, w_ukv, w_kr, g_k_nope, g_k_rope):
    b, s, d = x.shape
    depth = norm_g.shape[0]
    n_conv = conv_w_in.shape[0]
    t = b * s
    tm = min(TOKEN_TILE, s)
    blk = min(FLASH_BLOCK, s)
    assert s % blk == 0 and blk % tm == 0 and d - conv_w.shape[2] == MEM_WIDTH

    head_of = jnp.arange(MEM_WIDTH, dtype=jnp.int32) // MEM_HEAD_DIM
    ones_bd = (head_of[:, None] == head_of[None, :]).astype(BF16)
    kbd, vbd = _mem_kv(mem, mem_norm_g, w_mem_kv, g_mem_k, ones_bd)
    gq_all = jnp.tile(g_mem_q, (1, MEM_HEADS)).reshape(depth, 1, MEM_WIDTH)
    cos, sin = _rope_tables(positions)
    cos = cos.reshape(b, s, LANES)
    sin = sin.reshape(b, s, LANES)

    w13 = ffn_w13.astype(BF16)
    w2 = ffn_w2.astype(BF16)
    wo = w_out.astype(BF16)
    conv_in = conv_w_in.astype(BF16)
    mla_in = mla_w_in.astype(BF16)
    gn = norm_g.reshape(depth, norm_g.shape[1], 1, d)

    xf = x.reshape(t, d)
    shared = None
    for layer in range(depth):
        if layer == n_conv:
            shared = _kv_prep(xf.reshape(b, s, d), kv_norm_g, w_dkv, g_ckv, w_ukv, w_kr, g_k_nope,
                              g_k_rope, cos, sin, min(MIXER_TILE, blk), blk)
        xf = _ffn(xf, gn, w13, w2, layer, min(FFN_TILE, s))
        x3 = xf.reshape(b, s, d)
        if layer < n_conv:
            ymix, ymem = _conv_mixer(x3, gn, conv_in, conv_w, kbd, vbd, gq_all, ones_bd, layer,
                                     min(MIXER_TILE, s))
        else:
            j = layer - n_conv
            q, ymem = _q_prep(x3, gn, mla_in, g_q_lora[j], w_uq[j], g_q_nope[j], g_q_rope[j],
                              cos, sin, kbd, vbd, gq_all, ones_bd, layer, j, min(MIXER_TILE, s))
            ymix = _flash(q, shared[0], shared[1], blk)
        xf = _proj_ffn(xf, ymix.reshape(t, -1), ymem.reshape(t, MEM_WIDTH), wo, gn, w13, w2,
                       layer, min(FFN_TILE, s))
    return xf.reshape(b, s, d)
```

```python
import functools
import math

import jax
import jax.numpy as jnp
from jax import lax
from jax.experimental import pallas as pl
from jax.experimental.pallas import tpu as pltpu

F32 = jnp.float32
BF16 = jnp.bfloat16

EPS = 1e-6
FFN_HALF = 0.5
MEM_HEADS = 4
MEM_HEAD_DIM = 64
MEM_WIDTH = MEM_HEADS * MEM_HEAD_DIM
CONV_WIDTH = 3
MLA_HEADS = 6
QK_NOPE_DIM = 128
QK_ROPE_DIM = 64
ROPE_HALF = QK_ROPE_DIM // 2
QK_HEAD_DIM = QK_NOPE_DIM + QK_ROPE_DIM
V_HEAD_DIM = 128
ROPE_THETA = 10000.0

LANES = 128
SUBLANES = 8
ROPE_GROUPS = LANES // ROPE_HALF
QK_PAD_DIM = 2 * LANES
Q_GROUP = 3 * LANES
BF16_SUBLANES = 2 * SUBLANES
VT_ROWS = V_HEAD_DIM + BF16_SUBLANES
V7X_VMEM_BYTES = 64 * 1024 * 1024
VMEM_LIMIT_BYTES = V7X_VMEM_BYTES - 8 * 1024 * 1024
MASK_VALUE = -1e30

TOKEN_TILE = 512
MIXER_TILE = 1024
SUB_TILES = 2
FFN_TILE = 1024
FFN_SUB_ROWS = 256
FLASH_BLOCK = 1024
FLASH_UNROLL = 4


def _compiler_params(n_axes):
    return pltpu.CompilerParams(
        dimension_semantics=("arbitrary",) * n_axes, vmem_limit_bytes=VMEM_LIMIT_BYTES)


def _resident(shape, lead=()):
    block = (None,) * len(lead) + tuple(shape[len(lead):])
    index = tuple(lead) + (0,) * (len(shape) - len(lead))
    return pl.BlockSpec(block, lambda *_: index, pipeline_mode=pl.Buffered(1))


def _dot(a, b):
    return jnp.dot(a, b, preferred_element_type=F32)


def _dot_nt(a, b):
    return lax.dot_general(a, b, (((1,), (1,)), ((), ())), preferred_element_type=F32)


def _rms(x, g):
    return x * lax.rsqrt(jnp.mean(x * x, axis=-1, keepdims=True) + EPS) * g


def _group_sum(sq, ones_bd):
    hi = sq.astype(BF16)
    lo = (sq - hi.astype(F32)).astype(BF16)
    return _dot(hi, ones_bd) + _dot(lo, ones_bd)


def _rope_tables_kernel(pos_ref, inv_freq_ref, cos_ref, sin_ref):
    tq = pos_ref.shape[0]
    group = lax.broadcasted_iota(jnp.int32, (tq, LANES), 1) // ROPE_HALF
    pos = pos_ref[...].astype(F32)
    pos_lanes = pos[:, ROPE_GROUPS - 1:ROPE_GROUPS]
    for i in range(ROPE_GROUPS - 2, -1, -1):
        pos_lanes = jnp.where(group == i, pos[:, i:i + 1], pos_lanes)
    ang = pos_lanes * inv_freq_ref[...]
    for table, out_ref in ((jnp.cos(ang), cos_ref), (jnp.sin(ang), sin_ref)):
        rolled = [table] + [pltpu.roll(table, ROPE_HALF * k, 1) for k in range(1, ROPE_GROUPS)]
        for i in range(ROPE_GROUPS):
            spread = rolled[(ROPE_GROUPS - 1 - i) % ROPE_GROUPS]
            for j in range(ROPE_GROUPS - 2, -1, -1):
                spread = jnp.where(group == j, rolled[(j - i) % ROPE_GROUPS], spread)
            out_ref[i] = spread


def _rope_tables(positions):
    t = positions.size
    quarter = t // ROPE_GROUPS
    tq = quarter if quarter <= 2 * TOKEN_TILE else TOKEN_TILE
    assert t % ROPE_GROUPS == 0 and quarter % tq == 0
    pos = positions.reshape(ROPE_GROUPS, quarter).T
    inv_freq = ROPE_THETA ** (-jnp.arange(0, QK_ROPE_DIM, 2, dtype=F32) / QK_ROPE_DIM)
    inv_freq = jnp.tile(inv_freq, ROPE_GROUPS).reshape(1, LANES)
    out = jax.ShapeDtypeStruct((ROPE_GROUPS, quarter, LANES), F32)
    out_spec = pl.BlockSpec((ROPE_GROUPS, tq, LANES), lambda i: (0, i, 0))
    cos, sin = pl.pallas_call(
        _rope_tables_kernel,
        grid=(quarter // tq,),
        in_specs=[pl.BlockSpec((tq, ROPE_GROUPS), lambda i: (i, 0)), _resident((1, LANES))],
        out_specs=[out_spec, out_spec],
        out_shape=[out, out],
        compiler_params=_compiler_params(1),
        name="rope_tables",
    )(pos, inv_freq)
    return cos.reshape(t, LANES), sin.reshape(t, LANES)


def _mem_kv_kernel(mem_ref, g_ref, w_ref, gk_ref, ones_ref, kbd_ref, vbd_ref):
    m = mem_ref.shape[1]
    hm = _rms(mem_ref[0], g_ref[0]).astype(BF16)
    kv = _dot(hm, w_ref[0])
    k = kv[:, :MEM_WIDTH]
    v = kv[:, MEM_WIDTH:]
    ms = _group_sum(k * k, ones_ref[...]) * (1.0 / MEM_HEAD_DIM)
    kn = k * lax.rsqrt(ms + EPS) * gk_ref[0] * (MEM_HEAD_DIM ** -0.5)
    head_of_lane = lax.broadcasted_iota(jnp.int32, (m, MEM_WIDTH), 1) // MEM_HEAD_DIM
    for h in range(MEM_HEADS):
        sel = head_of_lane == h
        kbd_ref[0, 0, h * m:(h + 1) * m, :] = jnp.where(sel, kn, 0.0).astype(BF16)
        vbd_ref[0, 0, h * m:(h + 1) * m, :] = jnp.where(sel, v, 0.0).astype(BF16)


def _mem_kv(mem, mem_norm_g, w_mem_kv, g_mem_k, ones_bd):
    b, m, d = mem.shape
    depth = w_mem_kv.shape[0]
    gk = jnp.tile(g_mem_k, (1, MEM_HEADS)).reshape(depth, 1, MEM_WIDTH)
    out = jax.ShapeDtypeStruct((depth, b, MEM_HEADS * m, MEM_WIDTH), BF16)
    out_spec = pl.BlockSpec((1, 1, MEM_HEADS * m, MEM_WIDTH), lambda l, i: (l, i, 0, 0))
    return pl.pallas_call(
        _mem_kv_kernel,
        grid=(depth, b),
        in_specs=[
            pl.BlockSpec((1, m, d), lambda l, i: (i, 0, 0)),
            pl.BlockSpec((1, 1, d), lambda l, i: (l, 0, 0)),
            pl.BlockSpec((1, d, 2 * MEM_WIDTH), lambda l, i: (l, 0, 0)),
            pl.BlockSpec((1, 1, MEM_WIDTH), lambda l, i: (l, 0, 0)),
            _resident((MEM_WIDTH, MEM_WIDTH)),
        ],
        out_specs=[out_spec, out_spec],
        out_shape=[out, out],
        compiler_params=_compiler_params(2),
        name="mem_kv",
    )(mem, mem_norm_g.reshape(depth, 1, d), w_mem_kv.astype(BF16), gk, ones_bd)


def _memory_attention(q, kbd, vbd, gq, ones_bd):
    tm = q.shape[0]
    m = kbd.shape[0] // MEM_HEADS
    ms = _group_sum(q * q, ones_bd) * (1.0 / MEM_HEAD_DIM)
    qn = (q * lax.rsqrt(ms + EPS) * gq).astype(BF16)
    s = _dot_nt(qn, kbd)
    probs = []
    denoms = []
    for h in range(MEM_HEADS):
        sh = s[:, h * m:(h + 1) * m]
        p = jnp.exp(sh - jnp.max(sh, axis=-1, keepdims=True))
        denoms.append(jnp.sum(p, axis=-1, keepdims=True))
        probs.append(p.astype(BF16))
    o = _dot(jnp.concatenate(probs, axis=-1), vbd)
    head_of_lane = lax.broadcasted_iota(jnp.int32, (tm, MEM_WIDTH), 1) // MEM_HEAD_DIM
    denom = denoms[MEM_HEADS - 1]
    for h in range(MEM_HEADS - 2, -1, -1):
        denom = jnp.where(head_of_lane == h, denoms[h], denom)
    return o / denom


def _swiglu_step(x, g_ref, w13_ref, w2_ref):
    d_ff = w2_ref.shape[0]
    h = _rms(x, g_ref[...]).astype(BF16)
    gu = _dot(h, w13_ref[...])
    gate = gu[:, :d_ff]
    up = gu[:, d_ff:]
    act = (gate * jax.nn.sigmoid(gate) * up).astype(BF16)
    return x + FFN_HALF * _dot(act, w2_ref[...])


def _ffn_kernel(x_ref, g_ref, w13_ref, w2_ref, o_ref):
    for r in range(x_ref.shape[0] // FFN_SUB_ROWS):
        rs = slice(r * FFN_SUB_ROWS, (r + 1) * FFN_SUB_ROWS)
        o_ref[rs, :] = _swiglu_step(x_ref[rs, :], g_ref, w13_ref, w2_ref)


def _proj_ffn_kernel(x_ref, ymix_ref, ymem_ref, wout_ref, g_ref, w13_ref, w2_ref, o_ref):
    mix = ymix_ref.shape[1]
    for r in range(x_ref.shape[0] // FFN_SUB_ROWS):
        rs = slice(r * FFN_SUB_ROWS, (r + 1) * FFN_SUB_ROWS)
        x1 = (x_ref[rs, :] + _dot(ymix_ref[rs, :], wout_ref[:mix, :])
              + _dot(ymem_ref[rs, :], wout_ref[mix:, :]))
        o_ref[rs, :] = _swiglu_step(x1, g_ref, w13_ref, w2_ref)


def _ffn(x, norm_g, w13, w2, layer, half, tm):
    t, d = x.shape
    row = pl.BlockSpec((tm, d), lambda i: (i, 0))
    return pl.pallas_call(
        _ffn_kernel,
        grid=(t // tm,),
        in_specs=[row, _resident(norm_g.shape, (layer, 2 * half)),
                  _resident(w13.shape, (layer, half)), _resident(w2.shape, (layer, half))],
        out_specs=row,
        out_shape=jax.ShapeDtypeStruct((t, d), F32),
        compiler_params=_compiler_params(1),
        name="ffn",
    )(x, norm_g, w13, w2)


def _proj_ffn(x, ymix, ymem, w_out, norm_g, w13, w2, layer, tm):
    t, d = x.shape
    mix = ymix.shape[1]
    row = pl.BlockSpec((tm, d), lambda i: (i, 0))
    return pl.pallas_call(
        _proj_ffn_kernel,
        grid=(t // tm,),
        in_specs=[
            row,
            pl.BlockSpec((tm, mix), lambda i: (i, 0)),
            pl.BlockSpec((tm, d - mix), lambda i: (i, 0)),
            _resident(w_out.shape, (layer,)), _resident(norm_g.shape, (layer, 2)),
            _resident(w13.shape, (layer, 1)), _resident(w2.shape, (layer, 1)),
        ],
        out_specs=row,
        out_shape=jax.ShapeDtypeStruct((t, d), F32),
        compiler_params=_compiler_params(1),
        name="proj_ffn",
    )(x, ymix, ymem, w_out, norm_g, w13, w2)


def _conv_mixer_kernel(x_ref, g_ref, win_ref, convw_ref, kbd_ref, vbd_ref, gq_ref, ones_ref,
                       wout_ref, o_ref, carry_ref):
    tm = x_ref.shape[1]
    ch = convw_ref.shape[1]

    @pl.when(pl.program_id(1) == 0)
    def _():
        carry_ref[...] = jnp.zeros_like(carry_ref)

    sub = tm // SUB_TILES
    w = convw_ref[...]
    row = lax.broadcasted_iota(jnp.int32, (sub, ch), 0)
    tail = carry_ref[...]
    for r in range(SUB_TILES):
        rs = slice(r * sub, (r + 1) * sub)
        x = x_ref[0, rs, :]
        h = _rms(x, g_ref[...]).astype(BF16)
        q_mem = _dot(h, win_ref[:, 3 * ch:])
        y_mem = _memory_attention(
            q_mem, kbd_ref[0, 0], vbd_ref[0, 0], gq_ref[...], ones_ref[...]).astype(BF16)
        proj = _dot(h, win_ref[:, :3 * ch])
        gate_b = proj[:, :ch]
        u = proj[:, ch:2 * ch] * proj[:, 2 * ch:3 * ch]
        prev2 = tail[SUBLANES - 2:SUBLANES - 1, :]
        prev1 = tail[SUBLANES - 1:SUBLANES, :]
        u1 = jnp.where(row == 0, prev1, pltpu.roll(u, 1, 0))
        u2 = jnp.where(row == 0, prev2, jnp.where(row == 1, prev1, pltpu.roll(u, 2, 0)))
        tail = u[sub - SUBLANES:, :]
        conv = w[0:1, :] * u2 + w[1:2, :] * u1 + w[2:3, :] * u
        y_mix = (gate_b * conv).astype(BF16)
        o_ref[0, rs, :] = x + _dot(y_mix, wout_ref[:ch, :]) + _dot(y_mem, wout_ref[ch:, :])
    carry_ref[...] = tail


def _conv_mixer(x, norm_g, w_in, conv_w, kbd, vbd, gq, ones_bd, w_out, layer, tm):
    b, s, d = x.shape
    ch = conv_w.shape[2]
    mem_rows = kbd.shape[2]
    mem_spec = pl.BlockSpec((1, 1, mem_rows, MEM_WIDTH), lambda i, j: (layer, i, 0, 0))
    tile = pl.BlockSpec((1, tm, d), lambda i, j: (i, j, 0))
    return pl.pallas_call(
        _conv_mixer_kernel,
        grid=(b, s // tm),
        in_specs=[
            tile,
            _resident(norm_g.shape, (layer, 1)), _resident(w_in.shape, (layer,)),
            _resident(conv_w.shape, (layer,)),
            mem_spec, mem_spec, _resident(gq.shape, (layer,)), _resident((MEM_WIDTH, MEM_WIDTH)),
            _resident(w_out.shape, (layer,)),
        ],
        out_specs=tile,
        out_shape=jax.ShapeDtypeStruct((b, s, d), F32),
        scratch_shapes=[pltpu.VMEM((SUBLANES, ch), F32)],
        compiler_params=_compiler_params(2),
        name="conv_mixer",
    )(x, norm_g, w_in, conv_w, kbd, vbd, gq, ones_bd, w_out)


def _rope_rows(r, r_swapped, g, g_swapped, cos, sin):
    inv = lax.rsqrt(jnp.sum(r * r, axis=-1, keepdims=True) * (1.0 / QK_ROPE_DIM) + EPS)
    return (r * g * cos + r_swapped * g_swapped * sin) * inv


def _kv_prep_kernel(x_ref, g_ref, wd_ref, gckv_ref, wukv_ref, gkn_ref, gkr_ref, gkrs_ref,
                    cos_ref, sin_ref, k_ref, vt_ref):
    sub = x_ref.shape[1] // SUB_TILES
    lora = gckv_ref.shape[1]
    ones_rows = jnp.where(
        lax.broadcasted_iota(jnp.int32, (VT_ROWS - V_HEAD_DIM, sub), 0) == 0, 1.0, 0.0)
    for r in range(SUB_TILES):
        rs = slice(r * sub, (r + 1) * sub)
        h = _rms(x_ref[0, rs, :], g_ref[...]).astype(BF16)
        d = _dot(h, wd_ref[...])
        ckv = _rms(d[:, :lora], gckv_ref[...]).astype(BF16)
        kr = _rope_rows(d[:, lora:lora + LANES], d[:, lora + LANES:], gkr_ref[...], gkrs_ref[...],
                        cos_ref[0, rs, :], sin_ref[0, rs, :]).astype(BF16)
        kv = _dot(ckv, wukv_ref[...])
        for hd in range(MLA_HEADS):
            base = hd * (QK_NOPE_DIM + V_HEAD_DIM)
            kn = _rms(kv[:, base:base + QK_NOPE_DIM], gkn_ref[...])
            k_ref[0, hd, rs, :QK_NOPE_DIM] = kn.astype(BF16)
            k_ref[0, hd, rs, QK_NOPE_DIM:] = kr
            v = kv[:, base + QK_NOPE_DIM:base + QK_NOPE_DIM + V_HEAD_DIM]
            vt_ref[0, hd, 0, :, rs] = jnp.concatenate([v.T, ones_rows], axis=0).astype(BF16)


def _swap_rope_cols(w):
    pad = jnp.zeros(w.shape[:-1] + (LANES - QK_ROPE_DIM,), w.dtype)
    x1, x2 = w[..., :ROPE_HALF], w[..., ROPE_HALF:]
    return jnp.concatenate([w, pad], axis=-1), jnp.concatenate([-x2, x1, pad], axis=-1)


def _swap_rope_gain(g):
    pad = jnp.zeros((LANES - QK_ROPE_DIM,), g.dtype)
    straight = jnp.concatenate([g, pad])
    swapped = jnp.concatenate([g[ROPE_HALF:], g[:ROPE_HALF], pad])
    return straight.reshape(1, LANES), swapped.reshape(1, LANES)


def _kv_prep(x, kv_norm_g, w_dkv, g_ckv, w_ukv, w_kr, g_k_nope, g_k_rope, cos, sin, tm, blk):
    b, s, d = x.shape
    lora = w_dkv.shape[1]
    per_blk = blk // tm
    kr_cols, kr_swapped = _swap_rope_cols(w_kr)
    wd = jnp.concatenate([w_dkv, kr_cols, kr_swapped], axis=-1).astype(BF16)
    gkr, gkrs = _swap_rope_gain(g_k_rope)
    tab = pl.BlockSpec((1, tm, LANES), lambda i, j: (i, j, 0))
    return pl.pallas_call(
        _kv_prep_kernel,
        grid=(b, s // tm),
        in_specs=[
            pl.BlockSpec((1, tm, d), lambda i, j: (i, j, 0)),
            _resident((1, d)), _resident((d, lora + 2 * LANES)), _resident((1, lora)),
            _resident(w_ukv.shape), _resident((1, QK_NOPE_DIM)),
            _resident((1, LANES)), _resident((1, LANES)), tab, tab,
        ],
        out_specs=[
            pl.BlockSpec((1, MLA_HEADS, tm, QK_PAD_DIM), lambda i, j: (i, 0, j, 0)),
            pl.BlockSpec((1, MLA_HEADS, 1, VT_ROWS, tm),
                         lambda i, j: (i, 0, j // per_blk, 0, j % per_blk)),
        ],
        out_shape=[
            jax.ShapeDtypeStruct((b, MLA_HEADS, s, QK_PAD_DIM), BF16),
            jax.ShapeDtypeStruct((b, MLA_HEADS, s // blk, VT_ROWS, blk), BF16),
        ],
        compiler_params=_compiler_params(2),
        name="kv_prep",
    )(x, kv_norm_g.reshape(1, d), wd, g_ckv.reshape(1, lora), w_ukv.astype(BF16),
      g_k_nope.reshape(1, QK_NOPE_DIM), gkr, gkrs, cos, sin)


def _q_prep_kernel(x_ref, g_ref, win_ref, gql_ref, wuq_ref, gqn_ref, gqr_ref, gqrs_ref,
                   cos_ref, sin_ref, kbd_ref, vbd_ref, gq_ref, ones_ref, q_ref, ymem_ref,
                   *, score_scale):
    lora = gql_ref.shape[1]
    sub = x_ref.shape[1] // SUB_TILES
    for r in range(SUB_TILES):
        rs = slice(r * sub, (r + 1) * sub)
        h = _rms(x_ref[0, rs, :], g_ref[...]).astype(BF16)
        proj = _dot(h, win_ref[...])
        cq = _rms(proj[:, :lora], gql_ref[...]).astype(BF16)
        qall = _dot(cq, wuq_ref[...])
        cos = cos_ref[0, rs, :]
        sin = sin_ref[0, rs, :]
        for hd in range(MLA_HEADS):
            base = hd * Q_GROUP
            qn = _rms(qall[:, base:base + LANES], gqn_ref[...])
            qr = _rope_rows(qall[:, base + LANES:base + 2 * LANES],
                            qall[:, base + 2 * LANES:base + Q_GROUP],
                            gqr_ref[...], gqrs_ref[...], cos, sin)
            q_ref[0, hd, rs, :QK_NOPE_DIM] = (qn * score_scale).astype(BF16)
            q_ref[0, hd, rs, QK_NOPE_DIM:] = (qr * score_scale).astype(BF16)
        ymem_ref[0, rs, :] = _memory_attention(
            proj[:, lora:], kbd_ref[0, 0], vbd_ref[0, 0], gq_ref[...], ones_ref[...]).astype(BF16)


def _q_prep(x, norm_g, w_in, g_q_lora, w_uq, g_q_nope, g_q_rope, cos, sin, kbd, vbd, gq, ones_bd,
            layer, mla_layer, tm):
    b, s, d = x.shape
    lora = g_q_lora.shape[0]
    w = w_uq.reshape(lora, MLA_HEADS, QK_HEAD_DIM)
    rope_cols, rope_swapped = _swap_rope_cols(w[..., QK_NOPE_DIM:])
    wuq = jnp.concatenate([w[..., :QK_NOPE_DIM], rope_cols, rope_swapped], axis=-1)
    wuq = wuq.reshape(lora, MLA_HEADS * Q_GROUP).astype(BF16)
    gqr, gqrs = _swap_rope_gain(g_q_rope)
    mem_rows = kbd.shape[2]
    mem_spec = pl.BlockSpec((1, 1, mem_rows, MEM_WIDTH), lambda i, j: (layer, i, 0, 0))
    tab = pl.BlockSpec((1, tm, LANES), lambda i, j: (i, j, 0))
    score_scale = QK_HEAD_DIM ** -0.5 * math.log2(math.e)
    return pl.pallas_call(
        functools.partial(_q_prep_kernel, score_scale=score_scale),
        grid=(b, s // tm),
        in_specs=[
            pl.BlockSpec((1, tm, d), lambda i, j: (i, j, 0)),
            _resident(norm_g.shape, (layer, 1)), _resident(w_in.shape, (mla_layer,)),
            _resident((1, lora)),
            _resident((lora, MLA_HEADS * Q_GROUP)), _resident((1, QK_NOPE_DIM)),
            _resident((1, LANES)), _resident((1, LANES)), tab, tab,
            mem_spec, mem_spec, _resident(gq.shape, (layer,)), _resident((MEM_WIDTH, MEM_WIDTH)),
        ],
        out_specs=[
            pl.BlockSpec((1, MLA_HEADS, tm, QK_PAD_DIM), lambda i, j: (i, 0, j, 0)),
            pl.BlockSpec((1, tm, MEM_WIDTH), lambda i, j: (i, j, 0)),
        ],
        out_shape=[
            jax.ShapeDtypeStruct((b, MLA_HEADS, s, QK_PAD_DIM), BF16),
            jax.ShapeDtypeStruct((b, s, MEM_WIDTH), BF16),
        ],
        compiler_params=_compiler_params(2),
        name="q_prep",
    )(x, norm_g, w_in, g_q_lora.reshape(1, lora), wuq, g_q_nope.reshape(1, QK_NOPE_DIM),
      gqr, gqrs, cos, sin, kbd, vbd, gq, ones_bd)


def _flash_items(nq):
    items = [(i, i) for i in range(nq)]
    items += [(i, j) for j in range(nq - 1) for i in range(j + 1, nq)]
    return items


def _emit_steps(first, count, step, unroll=2):
    assert unroll % 2 == 0

    def body(u, carry):
        for k in range(unroll):
            step(first + unroll * u + k, (first + k) % 2)
        return carry

    trips = count // unroll
    if trips:
        lax.fori_loop(0, trips, body, 0)
    for t in range(first + trips * unroll, first + count):
        step(t, t % 2)


def _flash_kernel(qidx_ref, kidx_ref, q_ref, k_ref, vt_ref, o_ref,
                  s0_ref, s1_ref, mc0_ref, mc1_ref, m_ref, acc_ref, *, n_items):
    blk = s0_ref.shape[0]
    nq = m_ref.shape[0]
    s_bufs = (s0_ref, s1_ref)
    mc_bufs = (mc0_ref, mc1_ref)

    def rows(idx):
        return pl.ds(pl.multiple_of(idx * blk, blk), blk)

    half = blk // 2
    key_le_query = (lax.broadcasted_iota(jnp.int32, (half, half), 0)
                    <= lax.broadcasted_iota(jnp.int32, (half, half), 1))

    def half_rows(idx, which):
        return pl.ds(pl.multiple_of(idx * blk + which * half, half), half)

    def scores_full(t, slot):
        s = _dot_nt(k_ref[0, 0, rows(kidx_ref[t]), :], q_ref[0, 0, rows(qidx_ref[t]), :])
        s_bufs[slot][...] = s
        mc_bufs[slot][...] = jnp.max(s, axis=0, keepdims=True)

    def scores_diagonal(t, slot):
        i = qidx_ref[t]
        top = _dot_nt(k_ref[0, 0, half_rows(i, 0), :], q_ref[0, 0, rows(i), :])
        low = _dot_nt(k_ref[0, 0, half_rows(i, 1), :], q_ref[0, 0, half_rows(i, 1), :])
        top_left = jnp.where(key_le_query, top[:, :half], MASK_VALUE)
        low_right = jnp.where(key_le_query, low, MASK_VALUE)
        s_bufs[slot][:half, :half] = top_left
        s_bufs[slot][:half, half:] = top[:, half:]
        s_bufs[slot][half:, half:] = low_right
        mc_bufs[slot][:, :half] = jnp.max(top_left, axis=0, keepdims=True)
        mc_bufs[slot][:, half:] = jnp.maximum(jnp.max(top[:, half:], axis=0, keepdims=True),
                                              jnp.max(low_right, axis=0, keepdims=True))

    def absorb_full(t, slot):
        qi = qidx_ref[t]
        m_prev = m_ref[qi]
        m_new = jnp.maximum(m_prev, mc_bufs[slot][...])
        alpha = jnp.exp2(m_prev - m_new)
        p = jnp.exp2(s_bufs[slot][...] - m_new).astype(BF16)
        acc_ref[qi] = alpha * acc_ref[qi] + _dot(vt_ref[0, 0, kidx_ref[t]], p)
        m_ref[qi] = m_new

    def absorb_diagonal(t, slot):
        qi = qidx_ref[t]
        m_new = mc_bufs[slot][...]
        p_top = jnp.exp2(s_bufs[slot][:half, :] - m_new).astype(BF16)
        p_low = jnp.exp2(s_bufs[slot][half:, half:] - m_new[:, half:]).astype(BF16)
        vt = vt_ref[0, 0, qi]
        pv_top = _dot(vt[:, :half], p_top)
        pv_low = _dot(vt[:, half:], p_low)
        acc_ref[qi, :, :half] = pv_top[:, :half]
        acc_ref[qi, :, half:] = pv_top[:, half:] + pv_low
        m_ref[qi] = m_new

    def scores(t, slot, diagonal):
        (scores_diagonal if diagonal else scores_full)(t, slot)

    def absorb(t, slot, diagonal):
        (absorb_diagonal if diagonal else absorb_full)(t, slot)

    def step(absorb_is_diagonal, scores_is_diagonal):
        def run(t, slot):
            scores(t + 1, 1 - slot, scores_is_diagonal)
            absorb(t, slot, absorb_is_diagonal)
        return run

    scores(0, 0, True)
    _emit_steps(0, nq - 1, step(True, True))
    if n_items > nq:
        _emit_steps(nq - 1, 1, step(True, False))
        _emit_steps(nq, n_items - 1 - nq, step(False, False), unroll=FLASH_UNROLL)
    absorb(n_items - 1, (n_items - 1) % 2, n_items == nq)

    def finish(qi, carry):
        acc = acc_ref[qi]
        out_t = acc[:V_HEAD_DIM, :] / acc[V_HEAD_DIM:V_HEAD_DIM + 1, :]
        o_ref[0, rows(qi), :] = out_t.T.astype(BF16)
        return carry

    lax.fori_loop(0, nq, finish, 0)


def _flash(q, k, vt, blk):
    b, heads, s, _ = q.shape
    nq = s // blk
    items = _flash_items(nq)
    qidx = jnp.asarray([i for i, _ in items], jnp.int32)
    kidx = jnp.asarray([j for _, j in items], jnp.int32)
    grid_spec = pltpu.PrefetchScalarGridSpec(
        num_scalar_prefetch=2,
        grid=(b, heads),
        in_specs=[
            pl.BlockSpec((1, 1, s, QK_PAD_DIM), lambda i, h, *_: (i, h, 0, 0)),
            pl.BlockSpec((1, 1, s, QK_PAD_DIM), lambda i, h, *_: (i, h, 0, 0)),
            pl.BlockSpec((1, 1, nq, VT_ROWS, blk), lambda i, h, *_: (i, h, 0, 0, 0)),
        ],
        out_specs=pl.BlockSpec((1, s, V_HEAD_DIM), lambda i, h, *_: (i, 0, h)),
        scratch_shapes=[
            pltpu.VMEM((blk, blk), F32), pltpu.VMEM((blk, blk), F32),
            pltpu.VMEM((1, blk), F32), pltpu.VMEM((1, blk), F32),
            pltpu.VMEM((nq, 1, blk), F32), pltpu.VMEM((nq, VT_ROWS, blk), F32),
        ],
    )
    return pl.pallas_call(
        functools.partial(_flash_kernel, n_items=len(items)),
        grid_spec=grid_spec,
        out_shape=jax.ShapeDtypeStruct((b, s, heads * V_HEAD_DIM), BF16),
        compiler_params=_compiler_params(2),
        name="flash",
    )(qidx, kidx, q, k, vt)


def kernel(x, mem, positions, norm_g, ffn_w13, ffn_w2, w_out, mem_norm_g, w_mem_kv, g_mem_q, g_mem_k,
           conv_w_in, conv_w, mla_w_in, g_q_lora, w_uq, g_q_nope, g_q_rope,
           kv_norm_g, w_dkv, g_ckv, w_ukv, w_kr, g_k_nope, g_k_rope):
    b, s, d = x.shape
    depth = norm_g.shape[0]
    n_conv = conv_w_in.shape[0]
    t = b * s
    tm = min(TOKEN_TILE, s)
    blk = min(FLASH_BLOCK, s)
    assert s % blk == 0 and blk % tm == 0 and d - conv_w.shape[2] == MEM_WIDTH

    head_of = jnp.arange(MEM_WIDTH, dtype=jnp.int32) // MEM_HEAD_DIM
    ones_bd = (head_of[:, None] == head_of[None, :]).astype(BF16)
    kbd, vbd = _mem_kv(mem, mem_norm_g, w_mem_kv, g_mem_k, ones_bd)
    gq_all = jnp.tile(g_mem_q, (1, MEM_HEADS)).reshape(depth, 1, MEM_WIDTH)
    cos, sin = _rope_tables(positions)
    cos = cos.reshape(b, s, LANES)
    sin = sin.reshape(b, s, LANES)

    w13 = ffn_w13.astype(BF16)
    w2 = ffn_w2.astype(BF16)
    wo = w_out.astype(BF16)
    conv_in = conv_w_in.astype(BF16)
    mla_in = mla_w_in.astype(BF16)
    gn = norm_g.reshape(depth, norm_g.shape[1], 1, d)

    xf = x.reshape(t, d)
    shared = None
    for layer in range(depth):
        if layer == n_conv:
            shared = _kv_prep(xf.reshape(b, s, d), kv_norm_g, w_dkv, g_ckv, w_ukv, w_kr, g_k_nope,
                              g_k_rope, cos, sin, min(MIXER_TILE, blk), blk)
        xf = _ffn(xf, gn, w13, w2, layer, 0, min(FFN_TILE, s))
        x3 = xf.reshape(b, s, d)
        if layer < n_conv:
            x3 = _conv_mixer(x3, gn, conv_in, conv_w, kbd, vbd, gq_all, ones_bd, wo, layer,
                             min(MIXER_TILE, s))
            xf = _ffn(x3.reshape(t, d), gn, w13, w2, layer, 1, min(FFN_TILE, s))
        else:
            j = layer - n_conv
            q, ymem = _q_prep(x3, gn, mla_in, g_q_lora[j], w_uq[j], g_q_nope[j], g_q_rope[j],
                              cos, sin, kbd, vbd, gq_all, ones_bd, layer, j, min(MIXER_TILE, s))
            ymix = _flash(q, shared[0], shared[1], blk)
            xf = _proj_ffn(xf, ymix.reshape(t, -1), ymem.reshape(t, MEM_WIDTH), wo, gn, w13, w2,
                           layer, min(FFN_TILE, s))
    return xf.reshape(b, s, d)
```

```python
import functools
import math

import jax
import jax.numpy as jnp
from jax import lax
from jax.experimental import pallas as pl
from jax.experimental.pallas import tpu as pltpu

F32 = jnp.float32
BF16 = jnp.bfloat16

EPS = 1e-6
FFN_HALF = 0.5
MEM_HEADS = 4
MEM_HEAD_DIM = 64
MEM_WIDTH = MEM_HEADS * MEM_HEAD_DIM
CONV_WIDTH = 3
MLA_HEADS = 6
QK_NOPE_DIM = 128
QK_ROPE_DIM = 64
ROPE_HALF = QK_ROPE_DIM // 2
QK_HEAD_DIM = QK_NOPE_DIM + QK_ROPE_DIM
V_HEAD_DIM = 128
ROPE_THETA = 10000.0

LANES = 128
SUBLANES = 8
ROPE_GROUPS = LANES // ROPE_HALF
QK_PAD_DIM = 2 * LANES
Q_GROUP = 3 * LANES
BF16_SUBLANES = 2 * SUBLANES
VT_ROWS = V_HEAD_DIM + BF16_SUBLANES
V7X_VMEM_BYTES = 64 * 1024 * 1024
VMEM_LIMIT_BYTES = V7X_VMEM_BYTES - 8 * 1024 * 1024
MASK_VALUE = -1e30

TOKEN_TILE = 512
MIXER_TILE = 1024
SUB_TILES = 2
FFN_TILE = 1024
FFN_SUB_ROWS = 256
FLASH_BLOCK = 1024
FLASH_UNROLL = 4


def _compiler_params(n_axes):
    return pltpu.CompilerParams(
        dimension_semantics=("arbitrary",) * n_axes, vmem_limit_bytes=VMEM_LIMIT_BYTES)


def _resident(shape, lead=()):
    block = (None,) * len(lead) + tuple(shape[len(lead):])
    index = tuple(lead) + (0,) * (len(shape) - len(lead))
    return pl.BlockSpec(block, lambda *_: index, pipeline_mode=pl.Buffered(1))


def _dot(a, b):
    return jnp.dot(a, b, preferred_element_type=F32)


def _dot_nt(a, b):
    return lax.dot_general(a, b, (((1,), (1,)), ((), ())), preferred_element_type=F32)


def _dot_tn(a, b):
    return lax.dot_general(a, b, (((0,), (0,)), ((), ())), preferred_element_type=F32)


def _rms(x, g):
    return x * lax.rsqrt(jnp.mean(x * x, axis=-1, keepdims=True) + EPS) * g


def _group_sum(sq, ones_bd):
    hi = sq.astype(BF16)
    lo = (sq - hi.astype(F32)).astype(BF16)
    return _dot(hi, ones_bd) + _dot(lo, ones_bd)


def _rope_tables_kernel(pos_ref, inv_freq_ref, cos_ref, sin_ref):
    tq = pos_ref.shape[0]
    group = lax.broadcasted_iota(jnp.int32, (tq, LANES), 1) // ROPE_HALF
    pos = pos_ref[...].astype(F32)
    pos_lanes = pos[:, ROPE_GROUPS - 1:ROPE_GROUPS]
    for i in range(ROPE_GROUPS - 2, -1, -1):
        pos_lanes = jnp.where(group == i, pos[:, i:i + 1], pos_lanes)
    ang = pos_lanes * inv_freq_ref[...]
    for table, out_ref in ((jnp.cos(ang), cos_ref), (jnp.sin(ang), sin_ref)):
        rolled = [table] + [pltpu.roll(table, ROPE_HALF * k, 1) for k in range(1, ROPE_GROUPS)]
        for i in range(ROPE_GROUPS):
            spread = rolled[(ROPE_GROUPS - 1 - i) % ROPE_GROUPS]
            for j in range(ROPE_GROUPS - 2, -1, -1):
                spread = jnp.where(group == j, rolled[(j - i) % ROPE_GROUPS], spread)
            out_ref[i] = spread


def _rope_tables(positions):
    t = positions.size
    quarter = t // ROPE_GROUPS
    tq = quarter if quarter <= 2 * TOKEN_TILE else TOKEN_TILE
    assert t % ROPE_GROUPS == 0 and quarter % tq == 0
    pos = positions.reshape(ROPE_GROUPS, quarter).T
    inv_freq = ROPE_THETA ** (-jnp.arange(0, QK_ROPE_DIM, 2, dtype=F32) / QK_ROPE_DIM)
    inv_freq = jnp.tile(inv_freq, ROPE_GROUPS).reshape(1, LANES)
    out = jax.ShapeDtypeStruct((ROPE_GROUPS, quarter, LANES), F32)
    out_spec = pl.BlockSpec((ROPE_GROUPS, tq, LANES), lambda i: (0, i, 0))
    cos, sin = pl.pallas_call(
        _rope_tables_kernel,
        grid=(quarter // tq,),
        in_specs=[pl.BlockSpec((tq, ROPE_GROUPS), lambda i: (i, 0)), _resident((1, LANES))],
        out_specs=[out_spec, out_spec],
        out_shape=[out, out],
        compiler_params=_compiler_params(1),
        name="rope_tables",
    )(pos, inv_freq)
    return cos.reshape(t, LANES), sin.reshape(t, LANES)


def _mem_kv_kernel(mem_ref, g_ref, w_ref, gk_ref, ones_ref, kbd_ref, vbd_ref):
    m = mem_ref.shape[1]
    hm = _rms(mem_ref[0], g_ref[0]).astype(BF16)
    kv = _dot(hm, w_ref[0])
    k = kv[:, :MEM_WIDTH]
    v = kv[:, MEM_WIDTH:]
    ms = _group_sum(k * k, ones_ref[...]) * (1.0 / MEM_HEAD_DIM)
    kn = k * lax.rsqrt(ms + EPS) * gk_ref[0] * (MEM_HEAD_DIM ** -0.5)
    head_of_lane = lax.broadcasted_iota(jnp.int32, (m, MEM_WIDTH), 1) // MEM_HEAD_DIM
    for h in range(MEM_HEADS):
        sel = head_of_lane == h
        kbd_ref[0, 0, h * m:(h + 1) * m, :] = jnp.where(sel, kn, 0.0).astype(BF16)
        vbd_ref[0, 0, h * m:(h + 1) * m, :] = jnp.where(sel, v, 0.0).astype(BF16)


def _mem_kv(mem, mem_norm_g, w_mem_kv, g_mem_k, ones_bd):
    b, m, d = mem.shape
    depth = w_mem_kv.shape[0]
    gk = jnp.tile(g_mem_k, (1, MEM_HEADS)).reshape(depth, 1, MEM_WIDTH)
    out = jax.ShapeDtypeStruct((depth, b, MEM_HEADS * m, MEM_WIDTH), BF16)
    out_spec = pl.BlockSpec((1, 1, MEM_HEADS * m, MEM_WIDTH), lambda l, i: (l, i, 0, 0))
    return pl.pallas_call(
        _mem_kv_kernel,
        grid=(depth, b),
        in_specs=[
            pl.BlockSpec((1, m, d), lambda l, i: (i, 0, 0)),
            pl.BlockSpec((1, 1, d), lambda l, i: (l, 0, 0)),
            pl.BlockSpec((1, d, 2 * MEM_WIDTH), lambda l, i: (l, 0, 0)),
            pl.BlockSpec((1, 1, MEM_WIDTH), lambda l, i: (l, 0, 0)),
            _resident((MEM_WIDTH, MEM_WIDTH)),
        ],
        out_specs=[out_spec, out_spec],
        out_shape=[out, out],
        compiler_params=_compiler_params(2),
        name="mem_kv",
    )(mem, mem_norm_g.reshape(depth, 1, d), w_mem_kv.astype(BF16), gk, ones_bd)


def _memory_attention(q, kbd, vbd, gq, ones_bd):
    tm = q.shape[0]
    m = kbd.shape[0] // MEM_HEADS
    ms = _group_sum(q * q, ones_bd) * (1.0 / MEM_HEAD_DIM)
    qn = (q * lax.rsqrt(ms + EPS) * gq).astype(BF16)
    s = _dot_nt(qn, kbd)
    probs = []
    denoms = []
    for h in range(MEM_HEADS):
        sh = s[:, h * m:(h + 1) * m]
        p = jnp.exp(sh - jnp.max(sh, axis=-1, keepdims=True))
        denoms.append(jnp.sum(p, axis=-1, keepdims=True))
        probs.append(p.astype(BF16))
    o = _dot(jnp.concatenate(probs, axis=-1), vbd)
    head_of_lane = lax.broadcasted_iota(jnp.int32, (tm, MEM_WIDTH), 1) // MEM_HEAD_DIM
    denom = denoms[MEM_HEADS - 1]
    for h in range(MEM_HEADS - 2, -1, -1):
        denom = jnp.where(head_of_lane == h, denoms[h], denom)
    return o / denom


def _swiglu_step(x, g_ref, w13_ref, w2_ref):
    d_ff = w2_ref.shape[0]
    h = _rms(x, g_ref[...]).astype(BF16)
    gu = _dot(h, w13_ref[...])
    gate = gu[:, :d_ff]
    up = gu[:, d_ff:]
    act = (gate * jax.nn.sigmoid(gate) * up).astype(BF16)
    return x + FFN_HALF * _dot(act, w2_ref[...])


def _ffn_kernel(x_ref, g_ref, w13_ref, w2_ref, o_ref):
    for r in range(x_ref.shape[0] // FFN_SUB_ROWS):
        rs = slice(r * FFN_SUB_ROWS, (r + 1) * FFN_SUB_ROWS)
        o_ref[rs, :] = _swiglu_step(x_ref[rs, :], g_ref, w13_ref, w2_ref)


def _proj_ffn_kernel(x_ref, attn_t_ref, ymem_ref, wout_ref, g_ref, w13_ref, w2_ref, o_ref):
    heads, head_dim = attn_t_ref.shape[1], attn_t_ref.shape[3]
    mix = heads * head_dim
    for r in range(x_ref.shape[0] // FFN_SUB_ROWS):
        rs = slice(r * FFN_SUB_ROWS, (r + 1) * FFN_SUB_ROWS)
        attn_t = attn_t_ref[0, :, 0, :, rs].reshape(mix, FFN_SUB_ROWS)
        x1 = (x_ref[rs, :] + _dot_tn(attn_t, wout_ref[:mix, :])
              + _dot(ymem_ref[rs, :], wout_ref[mix:, :]))
        o_ref[rs, :] = _swiglu_step(x1, g_ref, w13_ref, w2_ref)


def _ffn(x, norm_g, w13, w2, layer, half, tm):
    t, d = x.shape
    row = pl.BlockSpec((tm, d), lambda i: (i, 0))
    return pl.pallas_call(
        _ffn_kernel,
        grid=(t // tm,),
        in_specs=[row, _resident(norm_g.shape, (layer, 2 * half)),
                  _resident(w13.shape, (layer, half)), _resident(w2.shape, (layer, half))],
        out_specs=row,
        out_shape=jax.ShapeDtypeStruct((t, d), F32),
        compiler_params=_compiler_params(1),
        name="ffn",
    )(x, norm_g, w13, w2)


def _proj_ffn(x, attn_t, ymem, w_out, norm_g, w13, w2, layer):
    t, d = x.shape
    _, heads, per_seq, head_dim, tm = attn_t.shape
    row = pl.BlockSpec((tm, d), lambda i: (i, 0))
    return pl.pallas_call(
        _proj_ffn_kernel,
        grid=(t // tm,),
        in_specs=[
            row,
            pl.BlockSpec((1, heads, 1, head_dim, tm),
                         lambda i: (i // per_seq, 0, i % per_seq, 0, 0)),
            pl.BlockSpec((tm, d - heads * head_dim), lambda i: (i, 0)),
            _resident(w_out.shape, (layer,)), _resident(norm_g.shape, (layer, 2)),
            _resident(w13.shape, (layer, 1)), _resident(w2.shape, (layer, 1)),
        ],
        out_specs=row,
        out_shape=jax.ShapeDtypeStruct((t, d), F32),
        compiler_params=_compiler_params(1),
        name="proj_ffn",
    )(x, attn_t, ymem, w_out, norm_g, w13, w2)


def _conv_mixer_kernel(x_ref, g_ref, win_ref, convw_ref, kbd_ref, vbd_ref, gq_ref, ones_ref,
                       wout_ref, o_ref, carry_ref):
    tm = x_ref.shape[1]
    ch = convw_ref.shape[1]

    @pl.when(pl.program_id(1) == 0)
    def _():
        carry_ref[...] = jnp.zeros_like(carry_ref)

    sub = tm // SUB_TILES
    w = convw_ref[...]
    row = lax.broadcasted_iota(jnp.int32, (sub, ch), 0)
    tail = carry_ref[...]
    for r in range(SUB_TILES):
        rs = slice(r * sub, (r + 1) * sub)
        x = x_ref[0, rs, :]
        h = _rms(x, g_ref[...]).astype(BF16)
        q_mem = _dot(h, win_ref[:, 3 * ch:])
        y_mem = _memory_attention(
            q_mem, kbd_ref[0, 0], vbd_ref[0, 0], gq_ref[...], ones_ref[...]).astype(BF16)
        proj = _dot(h, win_ref[:, :3 * ch])
        gate_b = proj[:, :ch]
        u = proj[:, ch:2 * ch] * proj[:, 2 * ch:3 * ch]
        prev2 = tail[SUBLANES - 2:SUBLANES - 1, :]
        prev1 = tail[SUBLANES - 1:SUBLANES, :]
        u1 = jnp.where(row == 0, prev1, pltpu.roll(u, 1, 0))
        u2 = jnp.where(row == 0, prev2, jnp.where(row == 1, prev1, pltpu.roll(u, 2, 0)))
        tail = u[sub - SUBLANES:, :]
        conv = w[0:1, :] * u2 + w[1:2, :] * u1 + w[2:3, :] * u
        y_mix = (gate_b * conv).astype(BF16)
        o_ref[0, rs, :] = x + _dot(y_mix, wout_ref[:ch, :]) + _dot(y_mem, wout_ref[ch:, :])
    carry_ref[...] = tail


def _conv_mixer(x, norm_g, w_in, conv_w, kbd, vbd, gq, ones_bd, w_out, layer, tm):
    b, s, d = x.shape
    ch = conv_w.shape[2]
    mem_rows = kbd.shape[2]
    mem_spec = pl.BlockSpec((1, 1, mem_rows, MEM_WIDTH), lambda i, j: (layer, i, 0, 0))
    tile = pl.BlockSpec((1, tm, d), lambda i, j: (i, j, 0))
    return pl.pallas_call(
        _conv_mixer_kernel,
        grid=(b, s // tm),
        in_specs=[
            tile,
            _resident(norm_g.shape, (layer, 1)), _resident(w_in.shape, (layer,)),
            _resident(conv_w.shape, (layer,)),
            mem_spec, mem_spec, _resident(gq.shape, (layer,)), _resident((MEM_WIDTH, MEM_WIDTH)),
            _resident(w_out.shape, (layer,)),
        ],
        out_specs=tile,
        out_shape=jax.ShapeDtypeStruct((b, s, d), F32),
        scratch_shapes=[pltpu.VMEM((SUBLANES, ch), F32)],
        compiler_params=_compiler_params(2),
        name="conv_mixer",
    )(x, norm_g, w_in, conv_w, kbd, vbd, gq, ones_bd, w_out)


def _rope_rows(r, r_swapped, g, g_swapped, cos, sin):
    inv = lax.rsqrt(jnp.sum(r * r, axis=-1, keepdims=True) * (1.0 / QK_ROPE_DIM) + EPS)
    return (r * g * cos + r_swapped * g_swapped * sin) * inv


def _kv_prep_kernel(x_ref, g_ref, wd_ref, gckv_ref, wukv_ref, gkn_ref, gkr_ref, gkrs_ref,
                    cos_ref, sin_ref, k_ref, vt_ref):
    sub = x_ref.shape[1] // SUB_TILES
    lora = gckv_ref.shape[1]
    ones_rows = jnp.where(
        lax.broadcasted_iota(jnp.int32, (VT_ROWS - V_HEAD_DIM, sub), 0) == 0, 1.0, 0.0)
    for r in range(SUB_TILES):
        rs = slice(r * sub, (r + 1) * sub)
        h = _rms(x_ref[0, rs, :], g_ref[...]).astype(BF16)
        d = _dot(h, wd_ref[...])
        ckv = _rms(d[:, :lora], gckv_ref[...]).astype(BF16)
        kr = _rope_rows(d[:, lora:lora + LANES], d[:, lora + LANES:], gkr_ref[...], gkrs_ref[...],
                        cos_ref[0, rs, :], sin_ref[0, rs, :]).astype(BF16)
        kv = _dot(ckv, wukv_ref[...])
        for hd in range(MLA_HEADS):
            base = hd * (QK_NOPE_DIM + V_HEAD_DIM)
            kn = _rms(kv[:, base:base + QK_NOPE_DIM], gkn_ref[...])
            k_ref[0, hd, rs, :QK_NOPE_DIM] = kn.astype(BF16)
            k_ref[0, hd, rs, QK_NOPE_DIM:] = kr
            v = kv[:, base + QK_NOPE_DIM:base + QK_NOPE_DIM + V_HEAD_DIM]
            vt_ref[0, hd, 0, :, rs] = jnp.concatenate([v.T, ones_rows], axis=0).astype(BF16)


def _swap_rope_cols(w):
    pad = jnp.zeros(w.shape[:-1] + (LANES - QK_ROPE_DIM,), w.dtype)
    x1, x2 = w[..., :ROPE_HALF], w[..., ROPE_HALF:]
    return jnp.concatenate([w, pad], axis=-1), jnp.concatenate([-x2, x1, pad], axis=-1)


def _swap_rope_gain(g):
    pad = jnp.zeros((LANES - QK_ROPE_DIM,), g.dtype)
    straight = jnp.concatenate([g, pad])
    swapped = jnp.concatenate([g[ROPE_HALF:], g[:ROPE_HALF], pad])
    return straight.reshape(1, LANES), swapped.reshape(1, LANES)


def _kv_prep(x, kv_norm_g, w_dkv, g_ckv, w_ukv, w_kr, g_k_nope, g_k_rope, cos, sin, tm, blk):
    b, s, d = x.shape
    lora = w_dkv.shape[1]
    per_blk = blk // tm
    kr_cols, kr_swapped = _swap_rope_cols(w_kr)
    wd = jnp.concatenate([w_dkv, kr_cols, kr_swapped], axis=-1).astype(BF16)
    gkr, gkrs = _swap_rope_gain(g_k_rope)
    tab = pl.BlockSpec((1, tm, LANES), lambda i, j: (i, j, 0))
    return pl.pallas_call(
        _kv_prep_kernel,
        grid=(b, s // tm),
        in_specs=[
            pl.BlockSpec((1, tm, d), lambda i, j: (i, j, 0)),
            _resident((1, d)), _resident((d, lora + 2 * LANES)), _resident((1, lora)),
            _resident(w_ukv.shape), _resident((1, QK_NOPE_DIM)),
            _resident((1, LANES)), _resident((1, LANES)), tab, tab,
        ],
        out_specs=[
            pl.BlockSpec((1, MLA_HEADS, tm, QK_PAD_DIM), lambda i, j: (i, 0, j, 0)),
            pl.BlockSpec((1, MLA_HEADS, 1, VT_ROWS, tm),
                         lambda i, j: (i, 0, j // per_blk, 0, j % per_blk)),
        ],
        out_shape=[
            jax.ShapeDtypeStruct((b, MLA_HEADS, s, QK_PAD_DIM), BF16),
            jax.ShapeDtypeStruct((b, MLA_HEADS, s // blk, VT_ROWS, blk), BF16),
        ],
        compiler_params=_compiler_params(2),
        name="kv_prep",
    )(x, kv_norm_g.reshape(1, d), wd, g_ckv.reshape(1, lora), w_ukv.astype(BF16),
      g_k_nope.reshape(1, QK_NOPE_DIM), gkr, gkrs, cos, sin)


def _q_prep_kernel(x_ref, g_ref, win_ref, gql_ref, wuq_ref, gqn_ref, gqr_ref, gqrs_ref,
                   cos_ref, sin_ref, kbd_ref, vbd_ref, gq_ref, ones_ref, q_ref, ymem_ref,
                   *, score_scale):
    lora = gql_ref.shape[1]
    sub = x_ref.shape[1] // SUB_TILES
    for r in range(SUB_TILES):
        rs = slice(r * sub, (r + 1) * sub)
        h = _rms(x_ref[0, rs, :], g_ref[...]).astype(BF16)
        proj = _dot(h, win_ref[...])
        cq = _rms(proj[:, :lora], gql_ref[...]).astype(BF16)
        qall = _dot(cq, wuq_ref[...])
        cos = cos_ref[0, rs, :]
        sin = sin_ref[0, rs, :]
        for hd in range(MLA_HEADS):
            base = hd * Q_GROUP
            qn = _rms(qall[:, base:base + LANES], gqn_ref[...])
            qr = _rope_rows(qall[:, base + LANES:base + 2 * LANES],
                            qall[:, base + 2 * LANES:base + Q_GROUP],
                            gqr_ref[...], gqrs_ref[...], cos, sin)
            q_ref[0, hd, rs, :QK_NOPE_DIM] = (qn * score_scale).astype(BF16)
            q_ref[0, hd, rs, QK_NOPE_DIM:] = (qr * score_scale).astype(BF16)
        ymem_ref[0, rs, :] = _memory_attention(
            proj[:, lora:], kbd_ref[0, 0], vbd_ref[0, 0], gq_ref[...], ones_ref[...]).astype(BF16)


def _q_prep(x, norm_g, w_in, g_q_lora, w_uq, g_q_nope, g_q_rope, cos, sin, kbd, vbd, gq, ones_bd,
            layer, mla_layer, tm):
    b, s, d = x.shape
    lora = g_q_lora.shape[0]
    w = w_uq.reshape(lora, MLA_HEADS, QK_HEAD_DIM)
    rope_cols, rope_swapped = _swap_rope_cols(w[..., QK_NOPE_DIM:])
    wuq = jnp.concatenate([w[..., :QK_NOPE_DIM], rope_cols, rope_swapped], axis=-1)
    wuq = wuq.reshape(lora, MLA_HEADS * Q_GROUP).astype(BF16)
    gqr, gqrs = _swap_rope_gain(g_q_rope)
    mem_rows = kbd.shape[2]
    mem_spec = pl.BlockSpec((1, 1, mem_rows, MEM_WIDTH), lambda i, j: (layer, i, 0, 0))
    tab = pl.BlockSpec((1, tm, LANES), lambda i, j: (i, j, 0))
    score_scale = QK_HEAD_DIM ** -0.5 * math.log2(math.e)
    return pl.pallas_call(
        functools.partial(_q_prep_kernel, score_scale=score_scale),
        grid=(b, s // tm),
        in_specs=[
            pl.BlockSpec((1, tm, d), lambda i, j: (i, j, 0)),
            _resident(norm_g.shape, (layer, 1)), _resident(w_in.shape, (mla_layer,)),
            _resident((1, lora)),
            _resident((lora, MLA_HEADS * Q_GROUP)), _resident((1, QK_NOPE_DIM)),
            _resident((1, LANES)), _resident((1, LANES)), tab, tab,
            mem_spec, mem_spec, _resident(gq.shape, (layer,)), _resident((MEM_WIDTH, MEM_WIDTH)),
        ],
        out_specs=[
            pl.BlockSpec((1, MLA_HEADS, tm, QK_PAD_DIM), lambda i, j: (i, 0, j, 0)),
            pl.BlockSpec((1, tm, MEM_WIDTH), lambda i, j: (i, j, 0)),
        ],
        out_shape=[
            jax.ShapeDtypeStruct((b, MLA_HEADS, s, QK_PAD_DIM), BF16),
            jax.ShapeDtypeStruct((b, s, MEM_WIDTH), BF16),
        ],
        compiler_params=_compiler_params(2),
        name="q_prep",
    )(x, norm_g, w_in, g_q_lora.reshape(1, lora), wuq, g_q_nope.reshape(1, QK_NOPE_DIM),
      gqr, gqrs, cos, sin, kbd, vbd, gq, ones_bd)


def _flash_items(nq):
    items = [(i, i) for i in range(nq)]
    items += [(i, j) for j in range(nq - 1) for i in range(j + 1, nq)]
    return items


def _emit_steps(first, count, step, unroll=2):
    assert unroll % 2 == 0

    def body(u, carry):
        for k in range(unroll):
            step(first + unroll * u + k, (first + k) % 2)
        return carry

    trips = count // unroll
    if trips:
        lax.fori_loop(0, trips, body, 0)
    for t in range(first + trips * unroll, first + count):
        step(t, t % 2)


def _flash_kernel(qidx_ref, kidx_ref, q_ref, k_ref, vt_ref, o_ref,
                  s0_ref, s1_ref, mc0_ref, mc1_ref, m_ref, acc_ref, *, n_items):
    blk = s0_ref.shape[0]
    nq = m_ref.shape[0]
    s_bufs = (s0_ref, s1_ref)
    mc_bufs = (mc0_ref, mc1_ref)

    def rows(idx):
        return pl.ds(pl.multiple_of(idx * blk, blk), blk)

    half = blk // 2
    key_le_query = (lax.broadcasted_iota(jnp.int32, (half, half), 0)
                    <= lax.broadcasted_iota(jnp.int32, (half, half), 1))

    def half_rows(idx, which):
        return pl.ds(pl.multiple_of(idx * blk + which * half, half), half)

    def scores_full(t, slot):
        s = _dot_nt(k_ref[0, 0, rows(kidx_ref[t]), :], q_ref[0, 0, rows(qidx_ref[t]), :])
        s_bufs[slot][...] = s
        mc_bufs[slot][...] = jnp.max(s, axis=0, keepdims=True)

    def scores_diagonal(t, slot):
        i = qidx_ref[t]
        top = _dot_nt(k_ref[0, 0, half_rows(i, 0), :], q_ref[0, 0, rows(i), :])
        low = _dot_nt(k_ref[0, 0, half_rows(i, 1), :], q_ref[0, 0, half_rows(i, 1), :])
        top_left = jnp.where(key_le_query, top[:, :half], MASK_VALUE)
        low_right = jnp.where(key_le_query, low, MASK_VALUE)
        s_bufs[slot][:half, :half] = top_left
        s_bufs[slot][:half, half:] = top[:, half:]
        s_bufs[slot][half:, half:] = low_right
        mc_bufs[slot][:, :half] = jnp.max(top_left, axis=0, keepdims=True)
        mc_bufs[slot][:, half:] = jnp.maximum(jnp.max(top[:, half:], axis=0, keepdims=True),
                                              jnp.max(low_right, axis=0, keepdims=True))

    def absorb_full(t, slot):
        qi = qidx_ref[t]
        m_prev = m_ref[qi]
        m_new = jnp.maximum(m_prev, mc_bufs[slot][...])
        alpha = jnp.exp2(m_prev - m_new)
        p = jnp.exp2(s_bufs[slot][...] - m_new).astype(BF16)
        acc_ref[qi] = alpha * acc_ref[qi] + _dot(vt_ref[0, 0, kidx_ref[t]], p)
        m_ref[qi] = m_new

    def absorb_diagonal(t, slot):
        qi = qidx_ref[t]
        m_new = mc_bufs[slot][...]
        p_top = jnp.exp2(s_bufs[slot][:half, :] - m_new).astype(BF16)
        p_low = jnp.exp2(s_bufs[slot][half:, half:] - m_new[:, half:]).astype(BF16)
        vt = vt_ref[0, 0, qi]
        pv_top = _dot(vt[:, :half], p_top)
        pv_low = _dot(vt[:, half:], p_low)
        acc_ref[qi, :, :half] = pv_top[:, :half]
        acc_ref[qi, :, half:] = pv_top[:, half:] + pv_low
        m_ref[qi] = m_new

    def scores(t, slot, diagonal):
        (scores_diagonal if diagonal else scores_full)(t, slot)

    def absorb(t, slot, diagonal):
        (absorb_diagonal if diagonal else absorb_full)(t, slot)

    def step(absorb_is_diagonal, scores_is_diagonal):
        def run(t, slot):
            scores(t + 1, 1 - slot, scores_is_diagonal)
            absorb(t, slot, absorb_is_diagonal)
        return run

    scores(0, 0, True)
    _emit_steps(0, nq - 1, step(True, True))
    if n_items > nq:
        _emit_steps(nq - 1, 1, step(True, False))
        _emit_steps(nq, n_items - 1 - nq, step(False, False), unroll=FLASH_UNROLL)
    absorb(n_items - 1, (n_items - 1) % 2, n_items == nq)

    def finish(qi, carry):
        acc = acc_ref[qi]
        out_t = acc[:V_HEAD_DIM, :] / acc[V_HEAD_DIM:V_HEAD_DIM + 1, :]
        o_ref[0, 0, qi] = out_t.astype(BF16)
        return carry

    lax.fori_loop(0, nq, finish, 0)


def _flash(q, k, vt, blk):
    b, heads, s, _ = q.shape
    nq = s // blk
    items = _flash_items(nq)
    qidx = jnp.asarray([i for i, _ in items], jnp.int32)
    kidx = jnp.asarray([j for _, j in items], jnp.int32)
    grid_spec = pltpu.PrefetchScalarGridSpec(
        num_scalar_prefetch=2,
        grid=(b, heads),
        in_specs=[
            pl.BlockSpec((1, 1, s, QK_PAD_DIM), lambda i, h, *_: (i, h, 0, 0)),
            pl.BlockSpec((1, 1, s, QK_PAD_DIM), lambda i, h, *_: (i, h, 0, 0)),
            pl.BlockSpec((1, 1, nq, VT_ROWS, blk), lambda i, h, *_: (i, h, 0, 0, 0)),
        ],
        out_specs=pl.BlockSpec((1, 1, nq, V_HEAD_DIM, blk), lambda i, h, *_: (i, h, 0, 0, 0)),
        scratch_shapes=[
            pltpu.VMEM((blk, blk), F32), pltpu.VMEM((blk, blk), F32),
            pltpu.VMEM((1, blk), F32), pltpu.VMEM((1, blk), F32),
            pltpu.VMEM((nq, 1, blk), F32), pltpu.VMEM((nq, VT_ROWS, blk), F32),
        ],
    )
    return pl.pallas_call(
        functools.partial(_flash_kernel, n_items=len(items)),
        grid_spec=grid_spec,
        out_shape=jax.ShapeDtypeStruct((b, heads, nq, V_HEAD_DIM, blk), BF16),
        compiler_params=_compiler_params(2),
        name="flash",
    )(qidx, kidx, q, k, vt)


def kernel(x, mem, positions, norm_g, ffn_w13, ffn_w2, w_out, mem_norm_g, w_mem_kv, g_mem_q, g_mem_k,
           conv_w_in, conv_w, mla_w_in, g_q_lora, w_uq, g_q_nope, g_q_rope,
           kv_norm_g, w_dkv, g_ckv, w_ukv, w_kr, g_k_nope, g_k_rope):
    b, s, d = x.shape
    depth = norm_g.shape[0]
    n_conv = conv_w_in.shape[0]
    t = b * s
    tm = min(TOKEN_TILE, s)
    blk = min(FLASH_BLOCK, s)
    assert s % blk == 0 and blk % tm == 0 and d - conv_w.shape[2] == MEM_WIDTH

    head_of = jnp.arange(MEM_WIDTH, dtype=jnp.int32) // MEM_HEAD_DIM
    ones_bd = (head_of[:, None] == head_of[None, :]).astype(BF16)
    kbd, vbd = _mem_kv(mem, mem_norm_g, w_mem_kv, g_mem_k, ones_bd)
    gq_all = jnp.tile(g_mem_q, (1, MEM_HEADS)).reshape(depth, 1, MEM_WIDTH)
    cos, sin = _rope_tables(positions)
    cos = cos.reshape(b, s, LANES)
    sin = sin.reshape(b, s, LANES)

    w13 = ffn_w13.astype(BF16)
    w2 = ffn_w2.astype(BF16)
    wo = w_out.astype(BF16)
    conv_in = conv_w_in.astype(BF16)
    mla_in = mla_w_in.astype(BF16)
    gn = norm_g.reshape(depth, norm_g.shape[1], 1, d)

    xf = x.reshape(t, d)
    shared = None
    for layer in range(depth):
        if layer == n_conv:
            shared = _kv_prep(xf.reshape(b, s, d), kv_norm_g, w_dkv, g_ckv, w_ukv, w_kr, g_k_nope,
                              g_k_rope, cos, sin, min(MIXER_TILE, blk), blk)
        xf = _ffn(xf, gn, w13, w2, layer, 0, min(FFN_TILE, s))
        x3 = xf.reshape(b, s, d)
        if layer < n_conv:
            x3 = _conv_mixer(x3, gn, conv_in, conv_w, kbd, vbd, gq_all, ones_bd, wo, layer,
                             min(MIXER_TILE, s))
            xf = _ffn(x3.reshape(t, d), gn, w13, w2, layer, 1, min(FFN_TILE, s))
        else:
            j = layer - n_conv
            q, ymem = _q_prep(x3, gn, mla_in, g_q_lora[j], w_uq[j], g_q_nope[j], g_q_rope[j],
                              cos, sin, kbd, vbd, gq_all, ones_bd, layer, j, min(MIXER_TILE, s))
            attn_t = _flash(q, shared[0], shared[1], blk)
            xf = _proj_ffn(xf, attn_t, ymem.reshape(t, MEM_WIDTH), wo, gn, w13, w2, layer)
    return xf.reshape(b, s, d)
```

```python
import functools
import math

import jax
import jax.numpy as jnp
from jax import lax
from jax.experimental import pallas as pl
from jax.experimental.pallas import tpu as pltpu

F32 = jnp.float32
BF16 = jnp.bfloat16

EPS = 1e-6
FFN_HALF = 0.5
MEM_HEADS = 4
MEM_HEAD_DIM = 64
MEM_WIDTH = MEM_HEADS * MEM_HEAD_DIM
CONV_WIDTH = 3
MLA_HEADS = 6
QK_NOPE_DIM = 128
QK_ROPE_DIM = 64
ROPE_HALF = QK_ROPE_DIM // 2
QK_HEAD_DIM = QK_NOPE_DIM + QK_ROPE_DIM
V_HEAD_DIM = 128
ROPE_THETA = 10000.0

LANES = 128
SUBLANES = 8
ROPE_GROUPS = LANES // ROPE_HALF
QK_PAD_DIM = 2 * LANES
Q_GROUP = 3 * LANES
BF16_SUBLANES = 2 * SUBLANES
VT_ROWS = V_HEAD_DIM + BF16_SUBLANES
V7X_VMEM_BYTES = 64 * 1024 * 1024
VMEM_LIMIT_BYTES = V7X_VMEM_BYTES - 8 * 1024 * 1024
MASK_VALUE = -1e30

TOKEN_TILE = 512
MIXER_TILE = 1024
SUB_TILES = 2
FFN_TILE = 1024
FFN_SUB_ROWS = 256
FLASH_BLOCK = 1024
FLASH_UNROLL = 4


def _compiler_params(n_axes):
    return pltpu.CompilerParams(
        dimension_semantics=("arbitrary",) * n_axes, vmem_limit_bytes=VMEM_LIMIT_BYTES)


def _resident(shape, lead=()):
    block = (None,) * len(lead) + tuple(shape[len(lead):])
    index = tuple(lead) + (0,) * (len(shape) - len(lead))
    return pl.BlockSpec(block, lambda *_: index, pipeline_mode=pl.Buffered(1))


def _dot(a, b):
    return jnp.dot(a, b, preferred_element_type=F32)


def _dot_nt(a, b):
    return lax.dot_general(a, b, (((1,), (1,)), ((), ())), preferred_element_type=F32)


def _dot_tn(a, b):
    return lax.dot_general(a, b, (((0,), (0,)), ((), ())), preferred_element_type=F32)


def _rms(x, g):
    return x * lax.rsqrt(jnp.mean(x * x, axis=-1, keepdims=True) + EPS) * g


def _group_sum(sq, ones_bd):
    hi = sq.astype(BF16)
    lo = (sq - hi.astype(F32)).astype(BF16)
    return _dot(hi, ones_bd) + _dot(lo, ones_bd)


def _rope_tables_kernel(pos_ref, inv_freq_ref, cos_ref, sin_ref):
    tq = pos_ref.shape[0]
    group = lax.broadcasted_iota(jnp.int32, (tq, LANES), 1) // ROPE_HALF
    pos = pos_ref[...].astype(F32)
    pos_lanes = pos[:, ROPE_GROUPS - 1:ROPE_GROUPS]
    for i in range(ROPE_GROUPS - 2, -1, -1):
        pos_lanes = jnp.where(group == i, pos[:, i:i + 1], pos_lanes)
    ang = pos_lanes * inv_freq_ref[...]
    for table, out_ref in ((jnp.cos(ang), cos_ref), (jnp.sin(ang), sin_ref)):
        rolled = [table] + [pltpu.roll(table, ROPE_HALF * k, 1) for k in range(1, ROPE_GROUPS)]
        for i in range(ROPE_GROUPS):
            spread = rolled[(ROPE_GROUPS - 1 - i) % ROPE_GROUPS]
            for j in range(ROPE_GROUPS - 2, -1, -1):
                spread = jnp.where(group == j, rolled[(j - i) % ROPE_GROUPS], spread)
            out_ref[i] = spread


def _rope_tables(positions):
    t = positions.size
    quarter = t // ROPE_GROUPS
    tq = quarter if quarter <= 2 * TOKEN_TILE else TOKEN_TILE
    assert t % ROPE_GROUPS == 0 and quarter % tq == 0
    pos = positions.reshape(ROPE_GROUPS, quarter).T
    inv_freq = ROPE_THETA ** (-jnp.arange(0, QK_ROPE_DIM, 2, dtype=F32) / QK_ROPE_DIM)
    inv_freq = jnp.tile(inv_freq, ROPE_GROUPS).reshape(1, LANES)
    out = jax.ShapeDtypeStruct((ROPE_GROUPS, quarter, LANES), F32)
    out_spec = pl.BlockSpec((ROPE_GROUPS, tq, LANES), lambda i: (0, i, 0))
    cos, sin = pl.pallas_call(
        _rope_tables_kernel,
        grid=(quarter // tq,),
        in_specs=[pl.BlockSpec((tq, ROPE_GROUPS), lambda i: (i, 0)), _resident((1, LANES))],
        out_specs=[out_spec, out_spec],
        out_shape=[out, out],
        compiler_params=_compiler_params(1),
        name="rope_tables",
    )(pos, inv_freq)
    return cos.reshape(t, LANES), sin.reshape(t, LANES)


def _mem_kv_kernel(mem_ref, g_ref, w_ref, gk_ref, ones_ref, kbd_ref, vbd_ref):
    m = mem_ref.shape[1]
    hm = _rms(mem_ref[0], g_ref[0]).astype(BF16)
    kv = _dot(hm, w_ref[0])
    k = kv[:, :MEM_WIDTH]
    v = kv[:, MEM_WIDTH:]
    ms = _group_sum(k * k, ones_ref[...]) * (1.0 / MEM_HEAD_DIM)
    kn = k * lax.rsqrt(ms + EPS) * gk_ref[0] * (MEM_HEAD_DIM ** -0.5)
    head_of_lane = lax.broadcasted_iota(jnp.int32, (m, MEM_WIDTH), 1) // MEM_HEAD_DIM
    for h in range(MEM_HEADS):
        sel = head_of_lane == h
        kbd_ref[0, 0, h * m:(h + 1) * m, :] = jnp.where(sel, kn, 0.0).astype(BF16)
        vbd_ref[0, 0, h * m:(h + 1) * m, :] = jnp.where(sel, v, 0.0).astype(BF16)


def _mem_kv(mem, mem_norm_g, w_mem_kv, g_mem_k, ones_bd):
    b, m, d = mem.shape
    depth = w_mem_kv.shape[0]
    gk = jnp.tile(g_mem_k, (1, MEM_HEADS)).reshape(depth, 1, MEM_WIDTH)
    out = jax.ShapeDtypeStruct((depth, b, MEM_HEADS * m, MEM_WIDTH), BF16)
    out_spec = pl.BlockSpec((1, 1, MEM_HEADS * m, MEM_WIDTH), lambda l, i: (l, i, 0, 0))
    return pl.pallas_call(
        _mem_kv_kernel,
        grid=(depth, b),
        in_specs=[
            pl.BlockSpec((1, m, d), lambda l, i: (i, 0, 0)),
            pl.BlockSpec((1, 1, d), lambda l, i: (l, 0, 0)),
            pl.BlockSpec((1, d, 2 * MEM_WIDTH), lambda l, i: (l, 0, 0)),
            pl.BlockSpec((1, 1, MEM_WIDTH), lambda l, i: (l, 0, 0)),
            _resident((MEM_WIDTH, MEM_WIDTH)),
        ],
        out_specs=[out_spec, out_spec],
        out_shape=[out, out],
        compiler_params=_compiler_params(2),
        name="mem_kv",
    )(mem, mem_norm_g.reshape(depth, 1, d), w_mem_kv.astype(BF16), gk, ones_bd)


def _memory_attention(q, kbd, vbd, gq, ones_bd):
    tm = q.shape[0]
    m = kbd.shape[0] // MEM_HEADS
    ms = _group_sum(q * q, ones_bd) * (1.0 / MEM_HEAD_DIM)
    qn = (q * lax.rsqrt(ms + EPS) * gq).astype(BF16)
    s = _dot_nt(qn, kbd)
    probs = []
    denoms = []
    for h in range(MEM_HEADS):
        sh = s[:, h * m:(h + 1) * m]
        p = jnp.exp(sh - jnp.max(sh, axis=-1, keepdims=True))
        denoms.append(jnp.sum(p, axis=-1, keepdims=True))
        probs.append(p.astype(BF16))
    o = _dot(jnp.concatenate(probs, axis=-1), vbd)
    head_of_lane = lax.broadcasted_iota(jnp.int32, (tm, MEM_WIDTH), 1) // MEM_HEAD_DIM
    denom = denoms[MEM_HEADS - 1]
    for h in range(MEM_HEADS - 2, -1, -1):
        denom = jnp.where(head_of_lane == h, denoms[h], denom)
    return o / denom


def _swiglu_step(x, g_ref, w13_ref, w2_ref):
    d_ff = w2_ref.shape[0]
    h = _rms(x, g_ref[...]).astype(BF16)
    gu = _dot(h, w13_ref[...])
    gate = gu[:, :d_ff]
    up = gu[:, d_ff:]
    act = (gate * jax.nn.sigmoid(gate) * up).astype(BF16)
    return x + FFN_HALF * _dot(act, w2_ref[...])


def _ffn_kernel(x_ref, g_ref, w13_ref, w2_ref, o_ref):
    for r in range(x_ref.shape[0] // FFN_SUB_ROWS):
        rs = slice(r * FFN_SUB_ROWS, (r + 1) * FFN_SUB_ROWS)
        o_ref[rs, :] = _swiglu_step(x_ref[rs, :], g_ref, w13_ref, w2_ref)


def _proj_ffn_kernel(x_ref, attn_t_ref, ymem_ref, wout_ref, g_ref, w13_ref, w2_ref, o_ref):
    heads, head_dim = attn_t_ref.shape[1], attn_t_ref.shape[3]
    mix = heads * head_dim
    for r in range(x_ref.shape[0] // FFN_SUB_ROWS):
        rs = slice(r * FFN_SUB_ROWS, (r + 1) * FFN_SUB_ROWS)
        attn_t = attn_t_ref[0, :, 0, :, rs].reshape(mix, FFN_SUB_ROWS)
        x1 = (x_ref[rs, :] + _dot_tn(attn_t, wout_ref[:mix, :])
              + _dot(ymem_ref[rs, :], wout_ref[mix:, :]))
        o_ref[rs, :] = _swiglu_step(x1, g_ref, w13_ref, w2_ref)


def _ffn(x, norm_g, w13, w2, layer, half, tm):
    t, d = x.shape
    row = pl.BlockSpec((tm, d), lambda i: (i, 0))
    return pl.pallas_call(
        _ffn_kernel,
        grid=(t // tm,),
        in_specs=[row, _resident(norm_g.shape, (layer, 2 * half)),
                  _resident(w13.shape, (layer, half)), _resident(w2.shape, (layer, half))],
        out_specs=row,
        out_shape=jax.ShapeDtypeStruct((t, d), F32),
        compiler_params=_compiler_params(1),
        name="ffn",
    )(x, norm_g, w13, w2)


def _proj_ffn(x, attn_t, ymem, w_out, norm_g, w13, w2, layer):
    t, d = x.shape
    _, heads, per_seq, head_dim, tm = attn_t.shape
    row = pl.BlockSpec((tm, d), lambda i: (i, 0))
    return pl.pallas_call(
        _proj_ffn_kernel,
        grid=(t // tm,),
        in_specs=[
            row,
            pl.BlockSpec((1, heads, 1, head_dim, tm),
                         lambda i: (i // per_seq, 0, i % per_seq, 0, 0)),
            pl.BlockSpec((tm, d - heads * head_dim), lambda i: (i, 0)),
            _resident(w_out.shape, (layer,)), _resident(norm_g.shape, (layer, 2)),
            _resident(w13.shape, (layer, 1)), _resident(w2.shape, (layer, 1)),
        ],
        out_specs=row,
        out_shape=jax.ShapeDtypeStruct((t, d), F32),
        compiler_params=_compiler_params(1),
        name="proj_ffn",
    )(x, attn_t, ymem, w_out, norm_g, w13, w2)


def _conv_mixer_kernel(x_ref, g_ref, win_ref, convw_ref, kbd_ref, vbd_ref, gq_ref, ones_ref,
                       wout_ref, o_ref, carry_ref):
    tm = x_ref.shape[1]
    ch = convw_ref.shape[1]

    @pl.when(pl.program_id(1) == 0)
    def _():
        carry_ref[...] = jnp.zeros_like(carry_ref)

    sub = tm // SUB_TILES
    w = convw_ref[...]
    row = lax.broadcasted_iota(jnp.int32, (sub, ch), 0)
    tail = carry_ref[...]
    for r in range(SUB_TILES):
        rs = slice(r * sub, (r + 1) * sub)
        x = x_ref[0, rs, :]
        h = _rms(x, g_ref[...]).astype(BF16)
        q_mem = _dot(h, win_ref[:, 3 * ch:])
        y_mem = _memory_attention(
            q_mem, kbd_ref[0, 0], vbd_ref[0, 0], gq_ref[...], ones_ref[...]).astype(BF16)
        proj = _dot(h, win_ref[:, :3 * ch])
        gate_b = proj[:, :ch]
        u = proj[:, ch:2 * ch] * proj[:, 2 * ch:3 * ch]
        prev2 = tail[SUBLANES - 2:SUBLANES - 1, :]
        prev1 = tail[SUBLANES - 1:SUBLANES, :]
        u1 = jnp.where(row == 0, prev1, pltpu.roll(u, 1, 0))
        u2 = jnp.where(row == 0, prev2, jnp.where(row == 1, prev1, pltpu.roll(u, 2, 0)))
        tail = u[sub - SUBLANES:, :]
        conv = w[0:1, :] * u2 + w[1:2, :] * u1 + w[2:3, :] * u
        y_mix = (gate_b * conv).astype(BF16)
        o_ref[0, rs, :] = x + _dot(y_mix, wout_ref[:ch, :]) + _dot(y_mem, wout_ref[ch:, :])
    carry_ref[...] = tail


def _conv_mixer(x, norm_g, w_in, conv_w, kbd, vbd, gq, ones_bd, w_out, layer, tm):
    b, s, d = x.shape
    ch = conv_w.shape[2]
    mem_rows = kbd.shape[2]
    mem_spec = pl.BlockSpec((1, 1, mem_rows, MEM_WIDTH), lambda i, j: (layer, i, 0, 0))
    tile = pl.BlockSpec((1, tm, d), lambda i, j: (i, j, 0))
    return pl.pallas_call(
        _conv_mixer_kernel,
        grid=(b, s // tm),
        in_specs=[
            tile,
            _resident(norm_g.shape, (layer, 1)), _resident(w_in.shape, (layer,)),
            _resident(conv_w.shape, (layer,)),
            mem_spec, mem_spec, _resident(gq.shape, (layer,)), _resident((MEM_WIDTH, MEM_WIDTH)),
            _resident(w_out.shape, (layer,)),
        ],
        out_specs=tile,
        out_shape=jax.ShapeDtypeStruct((b, s, d), F32),
        scratch_shapes=[pltpu.VMEM((SUBLANES, ch), F32)],
        compiler_params=_compiler_params(2),
        name="conv_mixer",
    )(x, norm_g, w_in, conv_w, kbd, vbd, gq, ones_bd, w_out)


def _rope_rows(r, r_swapped, g, g_swapped, cos, sin):
    inv = lax.rsqrt(jnp.sum(r * r, axis=-1, keepdims=True) * (1.0 / QK_ROPE_DIM) + EPS)
    return (r * g * cos + r_swapped * g_swapped * sin) * inv


def _kv_prep_kernel(x_ref, g_ref, wd_ref, gckv_ref, wukv_ref, gkn_ref, gkr_ref, gkrs_ref,
                    cos_ref, sin_ref, k_ref, vt_ref):
    sub = x_ref.shape[1] // SUB_TILES
    lora = gckv_ref.shape[1]
    ones_rows = jnp.where(
        lax.broadcasted_iota(jnp.int32, (VT_ROWS - V_HEAD_DIM, sub), 0) == 0, 1.0, 0.0)
    for r in range(SUB_TILES):
        rs = slice(r * sub, (r + 1) * sub)
        h = _rms(x_ref[0, rs, :], g_ref[...]).astype(BF16)
        d = _dot(h, wd_ref[...])
        ckv = _rms(d[:, :lora], gckv_ref[...]).astype(BF16)
        kr = _rope_rows(d[:, lora:lora + LANES], d[:, lora + LANES:], gkr_ref[...], gkrs_ref[...],
                        cos_ref[0, rs, :], sin_ref[0, rs, :]).astype(BF16)
        kv = _dot(ckv, wukv_ref[...])
        for hd in range(MLA_HEADS):
            base = hd * (QK_NOPE_DIM + V_HEAD_DIM)
            kn = _rms(kv[:, base:base + QK_NOPE_DIM], gkn_ref[...])
            k_ref[0, hd, rs, :QK_NOPE_DIM] = kn.astype(BF16)
            k_ref[0, hd, rs, QK_NOPE_DIM:] = kr
            v = kv[:, base + QK_NOPE_DIM:base + QK_NOPE_DIM + V_HEAD_DIM]
            vt_ref[0, hd, 0, :, rs] = jnp.concatenate([v.T, ones_rows], axis=0).astype(BF16)


def _swap_rope_cols(w):
    pad = jnp.zeros(w.shape[:-1] + (LANES - QK_ROPE_DIM,), w.dtype)
    x1, x2 = w[..., :ROPE_HALF], w[..., ROPE_HALF:]
    return jnp.concatenate([w, pad], axis=-1), jnp.concatenate([-x2, x1, pad], axis=-1)


def _swap_rope_gain(g):
    pad = jnp.zeros((LANES - QK_ROPE_DIM,), g.dtype)
    straight = jnp.concatenate([g, pad])
    swapped = jnp.concatenate([g[ROPE_HALF:], g[:ROPE_HALF], pad])
    return straight.reshape(1, LANES), swapped.reshape(1, LANES)


def _kv_prep(x, kv_norm_g, w_dkv, g_ckv, w_ukv, w_kr, g_k_nope, g_k_rope, cos, sin, tm, blk):
    b, s, d = x.shape
    lora = w_dkv.shape[1]
    per_blk = blk // tm
    kr_cols, kr_swapped = _swap_rope_cols(w_kr)
    wd = jnp.concatenate([w_dkv, kr_cols, kr_swapped], axis=-1).astype(BF16)
    gkr, gkrs = _swap_rope_gain(g_k_rope)
    tab = pl.BlockSpec((1, tm, LANES), lambda i, j: (i, j, 0))
    return pl.pallas_call(
        _kv_prep_kernel,
        grid=(b, s // tm),
        in_specs=[
            pl.BlockSpec((1, tm, d), lambda i, j: (i, j, 0)),
            _resident((1, d)), _resident((d, lora + 2 * LANES)), _resident((1, lora)),
            _resident(w_ukv.shape), _resident((1, QK_NOPE_DIM)),
            _resident((1, LANES)), _resident((1, LANES)), tab, tab,
        ],
        out_specs=[
            pl.BlockSpec((1, MLA_HEADS, tm, QK_PAD_DIM), lambda i, j: (i, 0, j, 0)),
            pl.BlockSpec((1, MLA_HEADS, 1, VT_ROWS, tm),
                         lambda i, j: (i, 0, j // per_blk, 0, j % per_blk)),
        ],
        out_shape=[
            jax.ShapeDtypeStruct((b, MLA_HEADS, s, QK_PAD_DIM), BF16),
            jax.ShapeDtypeStruct((b, MLA_HEADS, s // blk, VT_ROWS, blk), BF16),
        ],
        compiler_params=_compiler_params(2),
        name="kv_prep",
    )(x, kv_norm_g.reshape(1, d), wd, g_ckv.reshape(1, lora), w_ukv.astype(BF16),
      g_k_nope.reshape(1, QK_NOPE_DIM), gkr, gkrs, cos, sin)


def _q_prep_kernel(x_ref, g_ref, win_ref, gql_ref, wuq_ref, gqn_ref, gqr_ref, gqrs_ref,
                   cos_ref, sin_ref, kbd_ref, vbd_ref, gq_ref, ones_ref, q_ref, ymem_ref,
                   *, score_scale):
    lora = gql_ref.shape[1]
    sub = x_ref.shape[1] // SUB_TILES
    for r in range(SUB_TILES):
        rs = slice(r * sub, (r + 1) * sub)
        h = _rms(x_ref[0, rs, :], g_ref[...]).astype(BF16)
        proj = _dot(h, win_ref[...])
        cq = _rms(proj[:, :lora], gql_ref[...]).astype(BF16)
        qall = _dot(cq, wuq_ref[...])
        cos = cos_ref[0, rs, :]
        sin = sin_ref[0, rs, :]
        for hd in range(MLA_HEADS):
            base = hd * Q_GROUP
            qn = _rms(qall[:, base:base + LANES], gqn_ref[...])
            qr = _rope_rows(qall[:, base + LANES:base + 2 * LANES],
                            qall[:, base + 2 * LANES:base + Q_GROUP],
                            gqr_ref[...], gqrs_ref[...], cos, sin)
            q_ref[0, hd, rs, :QK_NOPE_DIM] = (qn * score_scale).astype(BF16)
            q_ref[0, hd, rs, QK_NOPE_DIM:] = (qr * score_scale).astype(BF16)
        ymem_ref[0, rs, :] = _memory_attention(
            proj[:, lora:], kbd_ref[0, 0], vbd_ref[0, 0], gq_ref[...], ones_ref[...]).astype(BF16)


def _q_prep(x, norm_g, w_in, g_q_lora, w_uq, g_q_nope, g_q_rope, cos, sin, kbd, vbd, gq, ones_bd,
            layer, mla_layer, tm):
    b, s, d = x.shape
    lora = g_q_lora.shape[0]
    w = w_uq.reshape(lora, MLA_HEADS, QK_HEAD_DIM)
    rope_cols, rope_swapped = _swap_rope_cols(w[..., QK_NOPE_DIM:])
    wuq = jnp.concatenate([w[..., :QK_NOPE_DIM], rope_cols, rope_swapped], axis=-1)
    wuq = wuq.reshape(lora, MLA_HEADS * Q_GROUP).astype(BF16)
    gqr, gqrs = _swap_rope_gain(g_q_rope)
    mem_rows = kbd.shape[2]
    mem_spec = pl.BlockSpec((1, 1, mem_rows, MEM_WIDTH), lambda i, j: (layer, i, 0, 0))
    tab = pl.BlockSpec((1, tm, LANES), lambda i, j: (i, j, 0))
    score_scale = QK_HEAD_DIM ** -0.5 * math.log2(math.e)
    return pl.pallas_call(
        functools.partial(_q_prep_kernel, score_scale=score_scale),
        grid=(b, s // tm),
        in_specs=[
            pl.BlockSpec((1, tm, d), lambda i, j: (i, j, 0)),
            _resident(norm_g.shape, (layer, 1)), _resident(w_in.shape, (mla_layer,)),
            _resident((1, lora)),
            _resident((lora, MLA_HEADS * Q_GROUP)), _resident((1, QK_NOPE_DIM)),
            _resident((1, LANES)), _resident((1, LANES)), tab, tab,
            mem_spec, mem_spec, _resident(gq.shape, (layer,)), _resident((MEM_WIDTH, MEM_WIDTH)),
        ],
        out_specs=[
            pl.BlockSpec((1, MLA_HEADS, tm, QK_PAD_DIM), lambda i, j: (i, 0, j, 0)),
            pl.BlockSpec((1, tm, MEM_WIDTH), lambda i, j: (i, j, 0)),
        ],
        out_shape=[
            jax.ShapeDtypeStruct((b, MLA_HEADS, s, QK_PAD_DIM), BF16),
            jax.ShapeDtypeStruct((b, s, MEM_WIDTH), BF16),
        ],
        compiler_params=_compiler_params(2),
        name="q_prep",
    )(x, norm_g, w_in, g_q_lora.reshape(1, lora), wuq, g_q_nope.reshape(1, QK_NOPE_DIM),
      gqr, gqrs, cos, sin, kbd, vbd, gq, ones_bd)


def _flash_items(nq):
    items, _ = _flash_plan(nq)
    return items


def _flash_plan(nq):
    bank_blocks = ([], [])
    load = [0, 0]
    for i in reversed(range(nq)):
        c = 0 if (load[0], len(bank_blocks[0])) <= (load[1], len(bank_blocks[1])) else 1
        bank_blocks[c].append(i)
        load[c] += i
    place = {i: (c, p) for c in range(2) for p, i in enumerate(bank_blocks[c])}
    if nq == 1:
        return [(0, 0)], place
    assert load[0] == load[1] and len(bank_blocks[0]) == len(bank_blocks[1])
    items = [(i, i) for pair in zip(*bank_blocks) for i in pair]
    full = [[(i, j) for j in range(nq - 1) for i in sorted(blocks) if i > j] for blocks in bank_blocks]
    items += [it for pair in zip(*full) for it in pair]
    return items, place


def _emit_steps(first, count, step, unroll=2):
    assert unroll % 2 == 0

    def body(u, carry):
        for k in range(unroll):
            step(first + unroll * u + k, (first + k) % 2)
        return carry

    trips = count // unroll
    if trips:
        lax.fori_loop(0, trips, body, 0)
    for t in range(first + trips * unroll, first + count):
        step(t, t % 2)


def _flash_kernel(qidx_ref, kidx_ref, sidx_ref, q_ref, k_ref, vt_ref, o_ref,
                  s0_ref, s1_ref, mc0_ref, mc1_ref, m0_ref, m1_ref, acc0_ref, acc1_ref,
                  *, n_items, place):
    blk = s0_ref.shape[0]
    nq = len(place)
    s_bufs = (s0_ref, s1_ref)
    mc_bufs = (mc0_ref, mc1_ref)
    m_banks = (m0_ref, m1_ref)
    acc_banks = (acc0_ref, acc1_ref)

    def rows(idx):
        return pl.ds(pl.multiple_of(idx * blk, blk), blk)

    half = blk // 2
    key_le_query = (lax.broadcasted_iota(jnp.int32, (half, half), 0)
                    <= lax.broadcasted_iota(jnp.int32, (half, half), 1))

    def half_rows(idx, which):
        return pl.ds(pl.multiple_of(idx * blk + which * half, half), half)

    def scores_full(t, slot):
        s = _dot_nt(k_ref[0, 0, rows(kidx_ref[t]), :], q_ref[0, 0, rows(qidx_ref[t]), :])
        s_bufs[slot][...] = s
        mc_bufs[slot][...] = jnp.max(s, axis=0, keepdims=True)

    def scores_diagonal(t, slot):
        i = qidx_ref[t]
        top = _dot_nt(k_ref[0, 0, half_rows(i, 0), :], q_ref[0, 0, rows(i), :])
        low = _dot_nt(k_ref[0, 0, half_rows(i, 1), :], q_ref[0, 0, half_rows(i, 1), :])
        top_left = jnp.where(key_le_query, top[:, :half], MASK_VALUE)
        low_right = jnp.where(key_le_query, low, MASK_VALUE)
        s_bufs[slot][:half, :half] = top_left
        s_bufs[slot][:half, half:] = top[:, half:]
        s_bufs[slot][half:, half:] = low_right
        mc_bufs[slot][:, :half] = jnp.max(top_left, axis=0, keepdims=True)
        mc_bufs[slot][:, half:] = jnp.maximum(jnp.max(top[:, half:], axis=0, keepdims=True),
                                              jnp.max(low_right, axis=0, keepdims=True))

    def absorb_full(t, slot):
        m_ref, acc_ref = m_banks[slot], acc_banks[slot]
        qi = sidx_ref[t]
        m_prev = m_ref[qi]
        m_new = jnp.maximum(m_prev, mc_bufs[slot][...])
        alpha = jnp.exp2(m_prev - m_new)
        p = jnp.exp2(s_bufs[slot][...] - m_new).astype(BF16)
        acc_ref[qi] = alpha * acc_ref[qi] + _dot(vt_ref[0, 0, kidx_ref[t]], p)
        m_ref[qi] = m_new

    def absorb_diagonal(t, slot):
        m_ref, acc_ref = m_banks[slot], acc_banks[slot]
        qi = sidx_ref[t]
        m_new = mc_bufs[slot][...]
        p_top = jnp.exp2(s_bufs[slot][:half, :] - m_new).astype(BF16)
        p_low = jnp.exp2(s_bufs[slot][half:, half:] - m_new[:, half:]).astype(BF16)
        vt = vt_ref[0, 0, kidx_ref[t]]
        pv_top = _dot(vt[:, :half], p_top)
        pv_low = _dot(vt[:, half:], p_low)
        acc_ref[qi, :, :half] = pv_top[:, :half]
        acc_ref[qi, :, half:] = pv_top[:, half:] + pv_low
        m_ref[qi] = m_new

    def scores(t, slot, diagonal):
        (scores_diagonal if diagonal else scores_full)(t, slot)

    def absorb(t, slot, diagonal):
        (absorb_diagonal if diagonal else absorb_full)(t, slot)

    def step(absorb_is_diagonal, scores_is_diagonal):
        def run(t, slot):
            scores(t + 1, 1 - slot, scores_is_diagonal)
            absorb(t, slot, absorb_is_diagonal)
        return run

    scores(0, 0, True)
    _emit_steps(0, nq - 1, step(True, True))
    if n_items > nq:
        _emit_steps(nq - 1, 1, step(True, False))
        _emit_steps(nq, n_items - 1 - nq, step(False, False), unroll=FLASH_UNROLL)
    absorb(n_items - 1, (n_items - 1) % 2, n_items == nq)

    for qi in range(nq):
        bank, pos = place[qi]
        acc = acc_banks[bank][pos]
        out_t = acc[:V_HEAD_DIM, :] / acc[V_HEAD_DIM:V_HEAD_DIM + 1, :]
        o_ref[0, 0, qi] = out_t.astype(BF16)


def _flash(q, k, vt, blk):
    b, heads, s, _ = q.shape
    nq = s // blk
    items, place = _flash_plan(nq)
    qidx = jnp.asarray([i for i, _ in items], jnp.int32)
    kidx = jnp.asarray([j for _, j in items], jnp.int32)
    sidx = jnp.asarray([place[i][1] for i, _ in items], jnp.int32)
    assert all(place[i][0] == t % 2 for t, (i, _) in enumerate(items))
    per_bank = (nq + 1) // 2
    grid_spec = pltpu.PrefetchScalarGridSpec(
        num_scalar_prefetch=3,
        grid=(b, heads),
        in_specs=[
            pl.BlockSpec((1, 1, s, QK_PAD_DIM), lambda i, h, *_: (i, h, 0, 0)),
            pl.BlockSpec((1, 1, s, QK_PAD_DIM), lambda i, h, *_: (i, h, 0, 0)),
            pl.BlockSpec((1, 1, nq, VT_ROWS, blk), lambda i, h, *_: (i, h, 0, 0, 0)),
        ],
        out_specs=pl.BlockSpec((1, 1, nq, V_HEAD_DIM, blk), lambda i, h, *_: (i, h, 0, 0, 0)),
        scratch_shapes=[
            pltpu.VMEM((blk, blk), F32), pltpu.VMEM((blk, blk), F32),
            pltpu.VMEM((1, blk), F32), pltpu.VMEM((1, blk), F32),
            pltpu.VMEM((per_bank, 1, blk), F32), pltpu.VMEM((per_bank, 1, blk), F32),
            pltpu.VMEM((per_bank, VT_ROWS, blk), F32), pltpu.VMEM((per_bank, VT_ROWS, blk), F32),
        ],
    )
    return pl.pallas_call(
        functools.partial(_flash_kernel, n_items=len(items),
                          place=tuple(place[i] for i in range(nq))),
        grid_spec=grid_spec,
        out_shape=jax.ShapeDtypeStruct((b, heads, nq, V_HEAD_DIM, blk), BF16),
        compiler_params=_compiler_params(2),
        name="flash",
    )(qidx, kidx, sidx, q, k, vt)


def kernel(x, mem, positions, norm_g, ffn_w13, ffn_w2, w_out, mem_norm_g, w_mem_kv, g_mem_q, g_mem_k,
           conv_w_in, conv_w, mla_w_in, g_q_lora, w_uq, g_q_nope, g_q_rope,
           kv_norm_g, w_dkv, g_ckv, w_ukv, w_kr, g_k_nope, g_k_rope):
    b, s, d = x.shape
    depth = norm_g.shape[0]
    n_conv = conv_w_in.shape[0]
    t = b * s
    tm = min(TOKEN_TILE, s)
    blk = min(FLASH_BLOCK, s)
    assert s % blk == 0 and blk % tm == 0 and d - conv_w.shape[2] == MEM_WIDTH

    head_of = jnp.arange(MEM_WIDTH, dtype=jnp.int32) // MEM_HEAD_DIM
    ones_bd = (head_of[:, None] == head_of[None, :]).astype(BF16)
    kbd, vbd = _mem_kv(mem, mem_norm_g, w_mem_kv, g_mem_k, ones_bd)
    gq_all = jnp.tile(g_mem_q, (1, MEM_HEADS)).reshape(depth, 1, MEM_WIDTH)
    cos, sin = _rope_tables(positions)
    cos = cos.reshape(b, s, LANES)
    sin = sin.reshape(b, s, LANES)

    w13 = ffn_w13.astype(BF16)
    w2 = ffn_w2.astype(BF16)
    wo = w_out.astype(BF16)
    conv_in = conv_w_in.astype(BF16)
    mla_in = mla_w_in.astype(BF16)
    gn = norm_g.reshape(depth, norm_g.shape[1], 1, d)

    xf = x.reshape(t, d)
    shared = None
    for layer in range(depth):
        if layer == n_conv:
            shared = _kv_prep(xf.reshape(b, s, d), kv_norm_g, w_dkv, g_ckv, w_ukv, w_kr, g_k_nope,
                              g_k_rope, cos, sin, min(MIXER_TILE, blk), blk)
        xf = _ffn(xf, gn, w13, w2, layer, 0, min(FFN_TILE, s))
        x3 = xf.reshape(b, s, d)
        if layer < n_conv:
            x3 = _conv_mixer(x3, gn, conv_in, conv_w, kbd, vbd, gq_all, ones_bd, wo, layer,
                             min(MIXER_TILE, s))
            xf = _ffn(x3.reshape(t, d), gn, w13, w2, layer, 1, min(FFN_TILE, s))
        else:
            j = layer - n_conv
            q, ymem = _q_prep(x3, gn, mla_in, g_q_lora[j], w_uq[j], g_q_nope[j], g_q_rope[j],
                              cos, sin, kbd, vbd, gq_all, ones_bd, layer, j, min(MIXER_TILE, s))
            attn_t = _flash(q, shared[0], shared[1], blk)
            xf = _proj_ffn(xf, attn_t, ymem.reshape(t, MEM_WIDTH), wo, gn, w13, w2, layer)
    return xf.reshape(b, s, d)
```

```python
import functools
import math

import jax
import jax.numpy as jnp
from jax import lax
from jax.experimental import pallas as pl
from jax.experimental.pallas import tpu as pltpu

F32 = jnp.float32
BF16 = jnp.bfloat16

EPS = 1e-6
FFN_HALF = 0.5
MEM_HEADS = 4
MEM_HEAD_DIM = 64
MEM_WIDTH = MEM_HEADS * MEM_HEAD_DIM
CONV_WIDTH = 3
MLA_HEADS = 6
QK_NOPE_DIM = 128
QK_ROPE_DIM = 64
ROPE_HALF = QK_ROPE_DIM // 2
QK_HEAD_DIM = QK_NOPE_DIM + QK_ROPE_DIM
V_HEAD_DIM = 128
ROPE_THETA = 10000.0

LANES = 128
SUBLANES = 8
ROPE_GROUPS = LANES // ROPE_HALF
QK_PAD_DIM = 2 * LANES
Q_GROUP = 3 * LANES
BF16_SUBLANES = 2 * SUBLANES
VT_ROWS = V_HEAD_DIM + BF16_SUBLANES
V7X_VMEM_BYTES = 64 * 1024 * 1024
VMEM_LIMIT_BYTES = V7X_VMEM_BYTES - 8 * 1024 * 1024
MASK_VALUE = -1e30

TOKEN_TILE = 512
MIXER_TILE = 1024
SUB_TILES = 2
FFN_TILE = 1024
FFN_SUB_ROWS = 256
FLASH_BLOCK = 1024
FLASH_UNROLL = 4


def _compiler_params(n_axes):
    return pltpu.CompilerParams(
        dimension_semantics=("arbitrary",) * n_axes, vmem_limit_bytes=VMEM_LIMIT_BYTES)


def _resident(shape, lead=()):
    block = (None,) * len(lead) + tuple(shape[len(lead):])
    index = tuple(lead) + (0,) * (len(shape) - len(lead))
    return pl.BlockSpec(block, lambda *_: index, pipeline_mode=pl.Buffered(1))


def _dot(a, b):
    return jnp.dot(a, b, preferred_element_type=F32)


def _dot_nt(a, b):
    return lax.dot_general(a, b, (((1,), (1,)), ((), ())), preferred_element_type=F32)


def _dot_tn(a, b):
    return lax.dot_general(a, b, (((0,), (0,)), ((), ())), preferred_element_type=F32)


def _rms(x, g):
    return x * lax.rsqrt(jnp.mean(x * x, axis=-1, keepdims=True) + EPS) * g


def _group_sum(sq, ones_bd):
    hi = sq.astype(BF16)
    lo = (sq - hi.astype(F32)).astype(BF16)
    return _dot(hi, ones_bd) + _dot(lo, ones_bd)


def _rope_tables_kernel(pos_ref, inv_freq_ref, cos_ref, sin_ref):
    tq = pos_ref.shape[0]
    group = lax.broadcasted_iota(jnp.int32, (tq, LANES), 1) // ROPE_HALF
    pos = pos_ref[...].astype(F32)
    pos_lanes = pos[:, ROPE_GROUPS - 1:ROPE_GROUPS]
    for i in range(ROPE_GROUPS - 2, -1, -1):
        pos_lanes = jnp.where(group == i, pos[:, i:i + 1], pos_lanes)
    ang = pos_lanes * inv_freq_ref[...]
    for table, out_ref in ((jnp.cos(ang), cos_ref), (jnp.sin(ang), sin_ref)):
        rolled = [table] + [pltpu.roll(table, ROPE_HALF * k, 1) for k in range(1, ROPE_GROUPS)]
        for i in range(ROPE_GROUPS):
            spread = rolled[(ROPE_GROUPS - 1 - i) % ROPE_GROUPS]
            for j in range(ROPE_GROUPS - 2, -1, -1):
                spread = jnp.where(group == j, rolled[(j - i) % ROPE_GROUPS], spread)
            out_ref[i] = spread


def _rope_tables(positions):
    t = positions.size
    quarter = t // ROPE_GROUPS
    tq = quarter if quarter <= 2 * TOKEN_TILE else TOKEN_TILE
    assert t % ROPE_GROUPS == 0 and quarter % tq == 0
    pos = positions.reshape(ROPE_GROUPS, quarter).T
    inv_freq = ROPE_THETA ** (-jnp.arange(0, QK_ROPE_DIM, 2, dtype=F32) / QK_ROPE_DIM)
    inv_freq = jnp.tile(inv_freq, ROPE_GROUPS).reshape(1, LANES)
    out = jax.ShapeDtypeStruct((ROPE_GROUPS, quarter, LANES), F32)
    out_spec = pl.BlockSpec((ROPE_GROUPS, tq, LANES), lambda i: (0, i, 0))
    cos, sin = pl.pallas_call(
        _rope_tables_kernel,
        grid=(quarter // tq,),
        in_specs=[pl.BlockSpec((tq, ROPE_GROUPS), lambda i: (i, 0)), _resident((1, LANES))],
        out_specs=[out_spec, out_spec],
        out_shape=[out, out],
        compiler_params=_compiler_params(1),
        name="rope_tables",
    )(pos, inv_freq)
    return cos.reshape(t, LANES), sin.reshape(t, LANES)


def _mem_kv_kernel(mem_ref, g_ref, w_ref, gk_ref, ones_ref, kbd_ref, vbd_ref):
    m = mem_ref.shape[1]
    hm = _rms(mem_ref[0], g_ref[0]).astype(BF16)
    kv = _dot(hm, w_ref[0])
    k = kv[:, :MEM_WIDTH]
    v = kv[:, MEM_WIDTH:]
    ms = _group_sum(k * k, ones_ref[...]) * (1.0 / MEM_HEAD_DIM)
    kn = k * lax.rsqrt(ms + EPS) * gk_ref[0] * (MEM_HEAD_DIM ** -0.5)
    head_of_lane = lax.broadcasted_iota(jnp.int32, (m, MEM_WIDTH), 1) // MEM_HEAD_DIM
    for h in range(MEM_HEADS):
        sel = head_of_lane == h
        kbd_ref[0, 0, h * m:(h + 1) * m, :] = jnp.where(sel, kn, 0.0).astype(BF16)
        vbd_ref[0, 0, h * m:(h + 1) * m, :] = jnp.where(sel, v, 0.0).astype(BF16)


def _mem_kv(mem, mem_norm_g, w_mem_kv, g_mem_k, ones_bd):
    b, m, d = mem.shape
    depth = w_mem_kv.shape[0]
    gk = jnp.tile(g_mem_k, (1, MEM_HEADS)).reshape(depth, 1, MEM_WIDTH)
    out = jax.ShapeDtypeStruct((depth, b, MEM_HEADS * m, MEM_WIDTH), BF16)
    out_spec = pl.BlockSpec((1, 1, MEM_HEADS * m, MEM_WIDTH), lambda l, i: (l, i, 0, 0))
    return pl.pallas_call(
        _mem_kv_kernel,
        grid=(depth, b),
        in_specs=[
            pl.BlockSpec((1, m, d), lambda l, i: (i, 0, 0)),
            pl.BlockSpec((1, 1, d), lambda l, i: (l, 0, 0)),
            pl.BlockSpec((1, d, 2 * MEM_WIDTH), lambda l, i: (l, 0, 0)),
            pl.BlockSpec((1, 1, MEM_WIDTH), lambda l, i: (l, 0, 0)),
            _resident((MEM_WIDTH, MEM_WIDTH)),
        ],
        out_specs=[out_spec, out_spec],
        out_shape=[out, out],
        compiler_params=_compiler_params(2),
        name="mem_kv",
    )(mem, mem_norm_g.reshape(depth, 1, d), w_mem_kv.astype(BF16), gk, ones_bd)


def _memory_attention(q, kbd, vbd, gq, ones_bd):
    tm = q.shape[0]
    m = kbd.shape[0] // MEM_HEADS
    ms = _group_sum(q * q, ones_bd) * (1.0 / MEM_HEAD_DIM)
    qn = (q * lax.rsqrt(ms + EPS) * gq).astype(BF16)
    s = _dot_nt(qn, kbd)
    probs = []
    denoms = []
    for h in range(MEM_HEADS):
        sh = s[:, h * m:(h + 1) * m]
        p = jnp.exp(sh - jnp.max(sh, axis=-1, keepdims=True))
        denoms.append(jnp.sum(p, axis=-1, keepdims=True))
        probs.append(p.astype(BF16))
    o = _dot(jnp.concatenate(probs, axis=-1), vbd)
    head_of_lane = lax.broadcasted_iota(jnp.int32, (tm, MEM_WIDTH), 1) // MEM_HEAD_DIM
    denom = denoms[MEM_HEADS - 1]
    for h in range(MEM_HEADS - 2, -1, -1):
        denom = jnp.where(head_of_lane == h, denoms[h], denom)
    return o / denom


def _swiglu_step(x, g_ref, w13_ref, w2_ref):
    d_ff = w2_ref.shape[0]
    h = _rms(x, g_ref[...]).astype(BF16)
    gu = _dot(h, w13_ref[...])
    gate = gu[:, :d_ff]
    up = gu[:, d_ff:]
    act = (gate * jax.nn.sigmoid(gate) * up).astype(BF16)
    return x + FFN_HALF * _dot(act, w2_ref[...])


def _ffn_kernel(x_ref, g_ref, w13_ref, w2_ref, o_ref):
    for r in range(x_ref.shape[0] // FFN_SUB_ROWS):
        rs = slice(r * FFN_SUB_ROWS, (r + 1) * FFN_SUB_ROWS)
        o_ref[rs, :] = _swiglu_step(x_ref[rs, :], g_ref, w13_ref, w2_ref)


def _proj_ffn_kernel(x_ref, attn_t_ref, ymem_ref, wout_ref, g_ref, w13_ref, w2_ref, o_ref):
    heads, head_dim = attn_t_ref.shape[1], attn_t_ref.shape[3]
    mix = heads * head_dim
    sub = 2 * FFN_SUB_ROWS
    for r in range(x_ref.shape[0] // sub):
        rs = slice(r * sub, (r + 1) * sub)
        attn_t = attn_t_ref[0, :, 0, :, rs].reshape(mix, sub)
        x1 = (x_ref[rs, :] + _dot_tn(attn_t, wout_ref[:mix, :])
              + _dot(ymem_ref[rs, :], wout_ref[mix:, :]))
        o_ref[rs, :] = _swiglu_step(x1, g_ref, w13_ref, w2_ref)


def _ffn(x, norm_g, w13, w2, layer, half, tm):
    t, d = x.shape
    row = pl.BlockSpec((tm, d), lambda i: (i, 0))
    return pl.pallas_call(
        _ffn_kernel,
        grid=(t // tm,),
        in_specs=[row, _resident(norm_g.shape, (layer, 2 * half)),
                  _resident(w13.shape, (layer, half)), _resident(w2.shape, (layer, half))],
        out_specs=row,
        out_shape=jax.ShapeDtypeStruct((t, d), F32),
        compiler_params=_compiler_params(1),
        name="ffn",
    )(x, norm_g, w13, w2)


def _proj_ffn(x, attn_t, ymem, w_out, norm_g, w13, w2, layer):
    t, d = x.shape
    _, heads, per_seq, head_dim, tm = attn_t.shape
    row = pl.BlockSpec((tm, d), lambda i: (i, 0))
    return pl.pallas_call(
        _proj_ffn_kernel,
        grid=(t // tm,),
        in_specs=[
            row,
            pl.BlockSpec((1, heads, 1, head_dim, tm),
                         lambda i: (i // per_seq, 0, i % per_seq, 0, 0)),
            pl.BlockSpec((tm, d - heads * head_dim), lambda i: (i, 0)),
            _resident(w_out.shape, (layer,)), _resident(norm_g.shape, (layer, 2)),
            _resident(w13.shape, (layer, 1)), _resident(w2.shape, (layer, 1)),
        ],
        out_specs=row,
        out_shape=jax.ShapeDtypeStruct((t, d), F32),
        compiler_params=_compiler_params(1),
        name="proj_ffn",
    )(x, attn_t, ymem, w_out, norm_g, w13, w2)


def _conv_mixer_kernel(x_ref, g_ref, win_ref, convw_ref, kbd_ref, vbd_ref, gq_ref, ones_ref,
                       wout_ref, o_ref, carry_ref):
    tm = x_ref.shape[1]
    ch = convw_ref.shape[1]

    @pl.when(pl.program_id(1) == 0)
    def _():
        carry_ref[...] = jnp.zeros_like(carry_ref)

    sub = tm // SUB_TILES
    w = convw_ref[...]
    row = lax.broadcasted_iota(jnp.int32, (sub, ch), 0)
    tail = carry_ref[...]
    for r in range(SUB_TILES):
        rs = slice(r * sub, (r + 1) * sub)
        x = x_ref[0, rs, :]
        h = _rms(x, g_ref[...]).astype(BF16)
        q_mem = _dot(h, win_ref[:, 3 * ch:])
        y_mem = _memory_attention(
            q_mem, kbd_ref[0, 0], vbd_ref[0, 0], gq_ref[...], ones_ref[...]).astype(BF16)
        proj = _dot(h, win_ref[:, :3 * ch])
        gate_b = proj[:, :ch]
        u = proj[:, ch:2 * ch] * proj[:, 2 * ch:3 * ch]
        prev2 = tail[SUBLANES - 2:SUBLANES - 1, :]
        prev1 = tail[SUBLANES - 1:SUBLANES, :]
        u1 = jnp.where(row == 0, prev1, pltpu.roll(u, 1, 0))
        u2 = jnp.where(row == 0, prev2, jnp.where(row == 1, prev1, pltpu.roll(u, 2, 0)))
        tail = u[sub - SUBLANES:, :]
        conv = w[0:1, :] * u2 + w[1:2, :] * u1 + w[2:3, :] * u
        y_mix = (gate_b * conv).astype(BF16)
        o_ref[0, rs, :] = x + _dot(y_mix, wout_ref[:ch, :]) + _dot(y_mem, wout_ref[ch:, :])
    carry_ref[...] = tail


def _conv_mixer(x, norm_g, w_in, conv_w, kbd, vbd, gq, ones_bd, w_out, layer, tm):
    b, s, d = x.shape
    ch = conv_w.shape[2]
    mem_rows = kbd.shape[2]
    mem_spec = pl.BlockSpec((1, 1, mem_rows, MEM_WIDTH), lambda i, j: (layer, i, 0, 0))
    tile = pl.BlockSpec((1, tm, d), lambda i, j: (i, j, 0))
    return pl.pallas_call(
        _conv_mixer_kernel,
        grid=(b, s // tm),
        in_specs=[
            tile,
            _resident(norm_g.shape, (layer, 1)), _resident(w_in.shape, (layer,)),
            _resident(conv_w.shape, (layer,)),
            mem_spec, mem_spec, _resident(gq.shape, (layer,)), _resident((MEM_WIDTH, MEM_WIDTH)),
            _resident(w_out.shape, (layer,)),
        ],
        out_specs=tile,
        out_shape=jax.ShapeDtypeStruct((b, s, d), F32),
        scratch_shapes=[pltpu.VMEM((SUBLANES, ch), F32)],
        compiler_params=_compiler_params(2),
        name="conv_mixer",
    )(x, norm_g, w_in, conv_w, kbd, vbd, gq, ones_bd, w_out)


def _rope_rows(r, r_swapped, g, g_swapped, cos, sin):
    inv = lax.rsqrt(jnp.sum(r * r, axis=-1, keepdims=True) * (1.0 / QK_ROPE_DIM) + EPS)
    return (r * g * cos + r_swapped * g_swapped * sin) * inv


def _kv_prep_kernel(x_ref, g_ref, wd_ref, gckv_ref, wuk_ref, wuvt_ref, gkn_ref, gkr_ref, gkrs_ref,
                    cos_ref, sin_ref, k_ref, vt_ref):
    sub = x_ref.shape[1] // SUB_TILES
    lora = gckv_ref.shape[1]
    ones_rows = jnp.where(
        lax.broadcasted_iota(jnp.int32, (VT_ROWS - V_HEAD_DIM, sub), 0) == 0, 1.0, 0.0)
    for r in range(SUB_TILES):
        rs = slice(r * sub, (r + 1) * sub)
        h = _rms(x_ref[0, rs, :], g_ref[...]).astype(BF16)
        d = _dot(h, wd_ref[...])
        ckv = _rms(d[:, :lora], gckv_ref[...]).astype(BF16)
        kr = _rope_rows(d[:, lora:lora + LANES], d[:, lora + LANES:], gkr_ref[...], gkrs_ref[...],
                        cos_ref[0, rs, :], sin_ref[0, rs, :]).astype(BF16)
        k_all = _dot(ckv, wuk_ref[...])
        vt_all = _dot_nt(wuvt_ref[...], ckv)
        for hd in range(MLA_HEADS):
            kn = _rms(k_all[:, hd * QK_NOPE_DIM:(hd + 1) * QK_NOPE_DIM], gkn_ref[...])
            k_ref[0, hd, rs, :QK_NOPE_DIM] = kn.astype(BF16)
            k_ref[0, hd, rs, QK_NOPE_DIM:] = kr
            vt = vt_all[hd * V_HEAD_DIM:(hd + 1) * V_HEAD_DIM, :]
            vt_ref[0, hd, 0, :, rs] = jnp.concatenate([vt, ones_rows], axis=0).astype(BF16)


def _swap_rope_cols(w):
    pad = jnp.zeros(w.shape[:-1] + (LANES - QK_ROPE_DIM,), w.dtype)
    x1, x2 = w[..., :ROPE_HALF], w[..., ROPE_HALF:]
    return jnp.concatenate([w, pad], axis=-1), jnp.concatenate([-x2, x1, pad], axis=-1)


def _swap_rope_gain(g):
    pad = jnp.zeros((LANES - QK_ROPE_DIM,), g.dtype)
    straight = jnp.concatenate([g, pad])
    swapped = jnp.concatenate([g[ROPE_HALF:], g[:ROPE_HALF], pad])
    return straight.reshape(1, LANES), swapped.reshape(1, LANES)


def _kv_prep(x, kv_norm_g, w_dkv, g_ckv, w_ukv, w_kr, g_k_nope, g_k_rope, cos, sin, tm, blk):
    b, s, d = x.shape
    lora = w_dkv.shape[1]
    per_blk = blk // tm
    kr_cols, kr_swapped = _swap_rope_cols(w_kr)
    wd = jnp.concatenate([w_dkv, kr_cols, kr_swapped], axis=-1).astype(BF16)
    gkr, gkrs = _swap_rope_gain(g_k_rope)
    w_heads = w_ukv.reshape(lora, MLA_HEADS, QK_NOPE_DIM + V_HEAD_DIM)
    w_uk = w_heads[..., :QK_NOPE_DIM].reshape(lora, MLA_HEADS * QK_NOPE_DIM).astype(BF16)
    w_uvt = w_heads[..., QK_NOPE_DIM:].reshape(lora, MLA_HEADS * V_HEAD_DIM).T.astype(BF16)
    tab = pl.BlockSpec((1, tm, LANES), lambda i, j: (i, j, 0))
    return pl.pallas_call(
        _kv_prep_kernel,
        grid=(b, s // tm),
        in_specs=[
            pl.BlockSpec((1, tm, d), lambda i, j: (i, j, 0)),
            _resident((1, d)), _resident((d, lora + 2 * LANES)), _resident((1, lora)),
            _resident(w_uk.shape), _resident(w_uvt.shape), _resident((1, QK_NOPE_DIM)),
            _resident((1, LANES)), _resident((1, LANES)), tab, tab,
        ],
        out_specs=[
            pl.BlockSpec((1, MLA_HEADS, tm, QK_PAD_DIM), lambda i, j: (i, 0, j, 0)),
            pl.BlockSpec((1, MLA_HEADS, 1, VT_ROWS, tm),
                         lambda i, j: (i, 0, j // per_blk, 0, j % per_blk)),
        ],
        out_shape=[
            jax.ShapeDtypeStruct((b, MLA_HEADS, s, QK_PAD_DIM), BF16),
            jax.ShapeDtypeStruct((b, MLA_HEADS, s // blk, VT_ROWS, blk), BF16),
        ],
        compiler_params=_compiler_params(2),
        name="kv_prep",
    )(x, kv_norm_g.reshape(1, d), wd, g_ckv.reshape(1, lora), w_uk, w_uvt,
      g_k_nope.reshape(1, QK_NOPE_DIM), gkr, gkrs, cos, sin)


def _q_prep_kernel(x_ref, g_ref, win_ref, gql_ref, wuq_ref, gqn_ref, gqr_ref, gqrs_ref,
                   cos_ref, sin_ref, kbd_ref, vbd_ref, gq_ref, ones_ref, q_ref, ymem_ref,
                   *, score_scale):
    lora = gql_ref.shape[1]
    sub = x_ref.shape[1] // SUB_TILES
    for r in range(SUB_TILES):
        rs = slice(r * sub, (r + 1) * sub)
        h = _rms(x_ref[0, rs, :], g_ref[...]).astype(BF16)
        proj = _dot(h, win_ref[...])
        cq = _rms(proj[:, :lora], gql_ref[...]).astype(BF16)
        qall = _dot(cq, wuq_ref[...])
        cos = cos_ref[0, rs, :]
        sin = sin_ref[0, rs, :]
        for hd in range(MLA_HEADS):
            base = hd * Q_GROUP
            qn = _rms(qall[:, base:base + LANES], gqn_ref[...])
            qr = _rope_rows(qall[:, base + LANES:base + 2 * LANES],
                            qall[:, base + 2 * LANES:base + Q_GROUP],
                            gqr_ref[...], gqrs_ref[...], cos, sin)
            q_ref[0, hd, rs, :QK_NOPE_DIM] = (qn * score_scale).astype(BF16)
            q_ref[0, hd, rs, QK_NOPE_DIM:] = (qr * score_scale).astype(BF16)
        ymem_ref[0, rs, :] = _memory_attention(
            proj[:, lora:], kbd_ref[0, 0], vbd_ref[0, 0], gq_ref[...], ones_ref[...]).astype(BF16)


def _q_prep(x, norm_g, w_in, g_q_lora, w_uq, g_q_nope, g_q_rope, cos, sin, kbd, vbd, gq, ones_bd,
            layer, mla_layer, tm):
    b, s, d = x.shape
    lora = g_q_lora.shape[0]
    w = w_uq.reshape(lora, MLA_HEADS, QK_HEAD_DIM)
    rope_cols, rope_swapped = _swap_rope_cols(w[..., QK_NOPE_DIM:])
    wuq = jnp.concatenate([w[..., :QK_NOPE_DIM], rope_cols, rope_swapped], axis=-1)
    wuq = wuq.reshape(lora, MLA_HEADS * Q_GROUP).astype(BF16)
    gqr, gqrs = _swap_rope_gain(g_q_rope)
    mem_rows = kbd.shape[2]
    mem_spec = pl.BlockSpec((1, 1, mem_rows, MEM_WIDTH), lambda i, j: (layer, i, 0, 0))
    tab = pl.BlockSpec((1, tm, LANES), lambda i, j: (i, j, 0))
    score_scale = QK_HEAD_DIM ** -0.5 * math.log2(math.e)
    return pl.pallas_call(
        functools.partial(_q_prep_kernel, score_scale=score_scale),
        grid=(b, s // tm),
        in_specs=[
            pl.BlockSpec((1, tm, d), lambda i, j: (i, j, 0)),
            _resident(norm_g.shape, (layer, 1)), _resident(w_in.shape, (mla_layer,)),
            _resident((1, lora)),
            _resident((lora, MLA_HEADS * Q_GROUP)), _resident((1, QK_NOPE_DIM)),
            _resident((1, LANES)), _resident((1, LANES)), tab, tab,
            mem_spec, mem_spec, _resident(gq.shape, (layer,)), _resident((MEM_WIDTH, MEM_WIDTH)),
        ],
        out_specs=[
            pl.BlockSpec((1, MLA_HEADS, tm, QK_PAD_DIM), lambda i, j: (i, 0, j, 0)),
            pl.BlockSpec((1, tm, MEM_WIDTH), lambda i, j: (i, j, 0)),
        ],
        out_shape=[
            jax.ShapeDtypeStruct((b, MLA_HEADS, s, QK_PAD_DIM), BF16),
            jax.ShapeDtypeStruct((b, s, MEM_WIDTH), BF16),
        ],
        compiler_params=_compiler_params(2),
        name="q_prep",
    )(x, norm_g, w_in, g_q_lora.reshape(1, lora), wuq, g_q_nope.reshape(1, QK_NOPE_DIM),
      gqr, gqrs, cos, sin, kbd, vbd, gq, ones_bd)


def _flash_items(nq):
    items = [(i, i) for i in range(nq)]
    items += [(i, j) for j in range(nq - 1) for i in range(j + 1, nq)]
    return items


def _emit_steps(first, count, step, unroll=2):
    assert unroll % 2 == 0

    def body(u, carry):
        for k in range(unroll):
            step(first + unroll * u + k, (first + k) % 2)
        return carry

    trips = count // unroll
    if trips:
        lax.fori_loop(0, trips, body, 0)
    for t in range(first + trips * unroll, first + count):
        step(t, t % 2)


def _flash_kernel(qidx_ref, kidx_ref, q_ref, k_ref, vt_ref, o_ref,
                  s0_ref, s1_ref, mc0_ref, mc1_ref, m_ref, acc_ref, *, n_items):
    blk = s0_ref.shape[0]
    nq = m_ref.shape[0]
    s_bufs = (s0_ref, s1_ref)
    mc_bufs = (mc0_ref, mc1_ref)

    def rows(idx):
        return pl.ds(pl.multiple_of(idx * blk, blk), blk)

    half = blk // 2
    key_le_query = (lax.broadcasted_iota(jnp.int32, (half, half), 0)
                    <= lax.broadcasted_iota(jnp.int32, (half, half), 1))

    def half_rows(idx, which):
        return pl.ds(pl.multiple_of(idx * blk + which * half, half), half)

    def scores_full(t, slot):
        s = _dot_nt(k_ref[0, 0, rows(kidx_ref[t]), :], q_ref[0, 0, rows(qidx_ref[t]), :])
        s_bufs[slot][...] = s
        mc_bufs[slot][...] = jnp.max(s, axis=0, keepdims=True)

    def scores_diagonal(t, slot):
        i = qidx_ref[t]
        top = _dot_nt(k_ref[0, 0, half_rows(i, 0), :], q_ref[0, 0, rows(i), :])
        low = _dot_nt(k_ref[0, 0, half_rows(i, 1), :], q_ref[0, 0, half_rows(i, 1), :])
        top_left = jnp.where(key_le_query, top[:, :half], MASK_VALUE)
        low_right = jnp.where(key_le_query, low, MASK_VALUE)
        s_bufs[slot][:half, :half] = top_left
        s_bufs[slot][:half, half:] = top[:, half:]
        s_bufs[slot][half:, half:] = low_right
        mc_bufs[slot][:, :half] = jnp.max(top_left, axis=0, keepdims=True)
        mc_bufs[slot][:, half:] = jnp.maximum(jnp.max(top[:, half:], axis=0, keepdims=True),
                                              jnp.max(low_right, axis=0, keepdims=True))

    def absorb_full(t, slot):
        qi = qidx_ref[t]
        m_prev = m_ref[qi]
        m_new = jnp.maximum(m_prev, mc_bufs[slot][...])
        alpha = jnp.exp2(m_prev - m_new)
        p = jnp.exp2(s_bufs[slot][...] - m_new).astype(BF16)
        acc_ref[qi] = alpha * acc_ref[qi] + _dot(vt_ref[0, 0, kidx_ref[t]], p)
        m_ref[qi] = m_new

    def absorb_diagonal(t, slot):
        qi = qidx_ref[t]
        m_new = mc_bufs[slot][...]
        p_top = jnp.exp2(s_bufs[slot][:half, :] - m_new).astype(BF16)
        p_low = jnp.exp2(s_bufs[slot][half:, half:] - m_new[:, half:]).astype(BF16)
        vt = vt_ref[0, 0, qi]
        pv_top = _dot(vt[:, :half], p_top)
        pv_low = _dot(vt[:, half:], p_low)
        acc_ref[qi, :, :half] = pv_top[:, :half]
        acc_ref[qi, :, half:] = pv_top[:, half:] + pv_low
        m_ref[qi] = m_new

    def scores(t, slot, diagonal):
        (scores_diagonal if diagonal else scores_full)(t, slot)

    def absorb(t, slot, diagonal):
        (absorb_diagonal if diagonal else absorb_full)(t, slot)

    def step(absorb_is_diagonal, scores_is_diagonal):
        def run(t, slot):
            scores(t + 1, 1 - slot, scores_is_diagonal)
            absorb(t, slot, absorb_is_diagonal)
        return run

    scores(0, 0, True)
    _emit_steps(0, nq - 1, step(True, True))
    if n_items > nq:
        _emit_steps(nq - 1, 1, step(True, False))
        _emit_steps(nq, n_items - 1 - nq, step(False, False), unroll=FLASH_UNROLL)
    absorb(n_items - 1, (n_items - 1) % 2, n_items == nq)

    def finish(qi, carry):
        acc = acc_ref[qi]
        out_t = acc[:V_HEAD_DIM, :] / acc[V_HEAD_DIM:V_HEAD_DIM + 1, :]
        o_ref[0, 0, qi] = out_t.astype(BF16)
        return carry

    lax.fori_loop(0, nq, finish, 0)


def _flash(q, k, vt, blk):
    b, heads, s, _ = q.shape
    nq = s // blk
    items = _flash_items(nq)
    qidx = jnp.asarray([i for i, _ in items], jnp.int32)
    kidx = jnp.asarray([j for _, j in items], jnp.int32)
    grid_spec = pltpu.PrefetchScalarGridSpec(
        num_scalar_prefetch=2,
        grid=(b, heads),
        in_specs=[
            pl.BlockSpec((1, 1, s, QK_PAD_DIM), lambda i, h, *_: (i, h, 0, 0)),
            pl.BlockSpec((1, 1, s, QK_PAD_DIM), lambda i, h, *_: (i, h, 0, 0)),
            pl.BlockSpec((1, 1, nq, VT_ROWS, blk), lambda i, h, *_: (i, h, 0, 0, 0)),
        ],
        out_specs=pl.BlockSpec((1, 1, nq, V_HEAD_DIM, blk), lambda i, h, *_: (i, h, 0, 0, 0)),
        scratch_shapes=[
            pltpu.VMEM((blk, blk), F32), pltpu.VMEM((blk, blk), F32),
            pltpu.VMEM((1, blk), F32), pltpu.VMEM((1, blk), F32),
            pltpu.VMEM((nq, 1, blk), F32), pltpu.VMEM((nq, VT_ROWS, blk), F32),
        ],
    )
    return pl.pallas_call(
        functools.partial(_flash_kernel, n_items=len(items)),
        grid_spec=grid_spec,
        out_shape=jax.ShapeDtypeStruct((b, heads, nq, V_HEAD_DIM, blk), BF16),
        compiler_params=_compiler_params(2),
        name="flash",
    )(qidx, kidx, q, k, vt)


def kernel(x, mem, positions, norm_g, ffn_w13, ffn_w2, w_out, mem_norm_g, w_mem_kv, g_mem_q, g_mem_k,
           conv_w_in, conv_w, mla_w_in, g_q_lora, w_uq, g_q_nope, g_q_rope,
           kv_norm_g, w_dkv, g_ckv, w_ukv, w_kr, g_k_nope, g_k_rope):
    b, s, d = x.shape
    depth = norm_g.shape[0]
    n_conv = conv_w_in.shape[0]
    t = b * s
    tm = min(TOKEN_TILE, s)
    blk = min(FLASH_BLOCK, s)
    assert s % blk == 0 and blk % tm == 0 and d - conv_w.shape[2] == MEM_WIDTH

    head_of = jnp.arange(MEM_WIDTH, dtype=jnp.int32) // MEM_HEAD_DIM
    ones_bd = (head_of[:, None] == head_of[None, :]).astype(BF16)
    kbd, vbd = _mem_kv(mem, mem_norm_g, w_mem_kv, g_mem_k, ones_bd)
    gq_all = jnp.tile(g_mem_q, (1, MEM_HEADS)).reshape(depth, 1, MEM_WIDTH)
    cos, sin = _rope_tables(positions)
    cos = cos.reshape(b, s, LANES)
    sin = sin.reshape(b, s, LANES)

    w13 = ffn_w13.astype(BF16)
    w2 = ffn_w2.astype(BF16)
    wo = w_out.astype(BF16)
    conv_in = conv_w_in.astype(BF16)
    mla_in = mla_w_in.astype(BF16)
    gn = norm_g.reshape(depth, norm_g.shape[1], 1, d)

    xf = x.reshape(t, d)
    shared = None
    for layer in range(depth):
        if layer == n_conv:
            shared = _kv_prep(xf.reshape(b, s, d), kv_norm_g, w_dkv, g_ckv, w_ukv, w_kr, g_k_nope,
                              g_k_rope, cos, sin, min(MIXER_TILE, blk), blk)
        xf = _ffn(xf, gn, w13, w2, layer, 0, min(FFN_TILE, s))
        x3 = xf.reshape(b, s, d)
        if layer < n_conv:
            x3 = _conv_mixer(x3, gn, conv_in, conv_w, kbd, vbd, gq_all, ones_bd, wo, layer,
                             min(MIXER_TILE, s))
            xf = _ffn(x3.reshape(t, d), gn, w13, w2, layer, 1, min(FFN_TILE, s))
        else:
            j = layer - n_conv
            q, ymem = _q_prep(x3, gn, mla_in, g_q_lora[j], w_uq[j], g_q_nope[j], g_q_rope[j],
                              cos, sin, kbd, vbd, gq_all, ones_bd, layer, j, min(MIXER_TILE, s))
            attn_t = _flash(q, shared[0], shared[1], blk)
            xf = _proj_ffn(xf, attn_t, ymem.reshape(t, MEM_WIDTH), wo, gn, w13, w2, layer)
    return xf.reshape(b, s, d)
```
